```python
import jax, jax.numpy as jnp
from jax import lax
import numpy as np

D_MODEL = 1024
BATCH = 4
SEQ = 8192
DEPTH = 2
DEC_BATCH = 8
DEC_SEQ = 32
PAST_LEN = 1024

CHUNK = 64
EPS = 1e-6
N_HEADS = 16
N_KV_HEADS = 2
GROUP = N_HEADS // N_KV_HEADS
HEAD_DIM = 64
WINDOW = 128
WINDOW_CHUNKS = WINDOW // CHUNK
ATTN_WIDTH = N_HEADS * HEAD_DIM
KV_WIDTH = N_KV_HEADS * HEAD_DIM
POOL_WINDOWS = (2, 4, 8, 16)
POOL_GROUPS = 4
POOL_GROUP_WIDTH = D_MODEL // 8
POOL_WIDTH = POOL_GROUPS * POOL_GROUP_WIDTH
POOL_STATE = 16 - 1
SGU_LEN = 128
SGU_GROUPS = 4
SGU_GROUP_WIDTH = D_MODEL // 8
SGU_WIDTH = SGU_GROUPS * SGU_GROUP_WIDTH
N_BRANCH = 3
Q_END = ATTN_WIDTH
K_END = Q_END + KV_WIDTH
V_END = K_END + KV_WIDTH
POOL_END = V_END + POOL_WIDTH
SU_END = POOL_END + SGU_WIDTH
SV_END = SU_END + SGU_WIDTH
N_IN = SV_END + N_BRANCH * D_MODEL
IN_SPLITS = (Q_END, K_END, V_END, POOL_END, SU_END, SV_END)
D_FF = 2816
CONV_W = 3

kernel_name = "chunk_causal_hybrid_gated_encoder_step"


def rmsnorm(x, g):
    xf = x.astype(jnp.float32)
    y = xf * lax.rsqrt(jnp.mean(xf * xf, axis=-1, keepdims=True) + EPS)
    return (y * g.astype(jnp.float32)).astype(x.dtype)


def alibi_slopes():
    h = jnp.arange(1, N_HEADS + 1, dtype=jnp.float32)
    return jnp.exp2(-8.0 * h / N_HEADS)


def band_attention(q, k, v, qpos, kpos, sink):
    logits = jnp.einsum('bnqkgd,bnskd->bnkgqs', q, k).astype(jnp.float32) * (HEAD_DIM ** -0.5)
    qc = (qpos // CHUNK)[:, :, None]
    kc = (kpos // CHUNK)[:, None, :]
    vis = (kpos[:, None, :] >= 0) & (kc <= qc) & (qc - kc <= WINDOW_CHUNKS)
    dist = jnp.abs(qpos[:, :, None] - kpos[:, None, :]).astype(jnp.float32)
    slopes = alibi_slopes().reshape(N_KV_HEADS, GROUP)
    bias = jnp.where(vis[:, None, None], -slopes[None, :, :, None, None] * dist[:, None, None], -jnp.inf)
    logits = logits + bias[None]
    s = sink.astype(jnp.float32).reshape(N_KV_HEADS, GROUP)[None, None, :, :, None, None]
    m = jnp.maximum(jnp.max(logits, axis=-1, keepdims=True), s)
    p = jnp.exp(logits - m)
    denom = jnp.sum(p, axis=-1, keepdims=True) + jnp.exp(s - m)
    return jnp.einsum('bnkgqs,bnskd->bnqkgd', (p / denom).astype(v.dtype), v)


def attn_prompt(q, k, v, sink):
    B, T = q.shape[:2]
    nc = T // CHUNK
    pad = WINDOW_CHUNKS * CHUNK

    def band(a):
        ap = jnp.pad(a, ((0, 0), (pad, 0), (0, 0), (0, 0))).reshape(B, nc + WINDOW_CHUNKS, CHUNK, N_KV_HEADS, HEAD_DIM)
        return jnp.concatenate([ap[:, j:j + nc] for j in range(WINDOW_CHUNKS + 1)], axis=2)

    qb = q.reshape(B, nc, CHUNK, N_KV_HEADS, GROUP, HEAD_DIM)
    qpos = jnp.arange(T, dtype=jnp.int32).reshape(nc, CHUNK)
    kpos = (jnp.arange(nc, dtype=jnp.int32) * CHUNK - pad)[:, None] + jnp.arange((WINDOW_CHUNKS + 1) * CHUNK, dtype=jnp.int32)[None, :]
    o = band_attention(qb, band(k), band(v), qpos, kpos, sink)
    return o.reshape(B, T, ATTN_WIDTH)


def attn_sample(q, k, v, win_k, win_v, pos0, sink):
    B, T = q.shape[:2]
    wc = win_k.shape[1]
    k_all = jnp.concatenate([win_k.astype(k.dtype), k], axis=1)
    v_all = jnp.concatenate([win_v.astype(v.dtype), v], axis=1)
    qpos = (pos0 + jnp.arange(T, dtype=jnp.int32))[None]
    kpos = (pos0 - wc + jnp.arange(wc + T, dtype=jnp.int32))[None]
    o = band_attention(q[:, None], k_all[:, None], v_all[:, None], qpos, kpos, sink)
    return o.reshape(B, T, ATTN_WIDTH), k_all[:, T:], v_all[:, T:]


def pool_mixer(u_ext, pos0, pool_w, pool_scale):
    B, L, _ = u_ext.shape
    T = L - POOL_STATE
    uf = u_ext.astype(jnp.float32)
    cs = jnp.concatenate([jnp.zeros((B, 1, POOL_WIDTH), jnp.float32), jnp.cumsum(uf, axis=1)], axis=1)
    pos = pos0 + jnp.arange(T, dtype=jnp.int32)
    outs = []
    for gi, w in enumerate(POOL_WINDOWS):
        c0, c1 = gi * POOL_GROUP_WIDTH, (gi + 1) * POOL_GROUP_WIDTH
        win_sum = cs[:, POOL_STATE + 1:POOL_STATE + 1 + T, c0:c1] - cs[:, POOL_STATE + 1 - w:POOL_STATE + 1 - w + T, c0:c1]
        cnt = jnp.minimum(pos + 1, w).astype(jnp.float32)[None, :, None]
        d = (win_sum / cnt - uf[:, POOL_STATE:, c0:c1]).astype(u_ext.dtype)
        outs.append(jnp.einsum('btc,cd->btd', d, pool_w[gi]))
    return jnp.concatenate(outs, axis=-1) * pool_scale


def spatial_gate(u, vn, sgu_w, sgu_b):
    L = u.shape[2]
    i = jnp.arange(L)
    mask = (i[None, :] // CHUNK) <= (i[:, None] // CHUNK)
    w = jnp.where(mask[None], sgu_w[:, :L, :L], 0)
    s = jnp.einsum('gij,bnjgc->bnigc', w, vn) + jnp.transpose(sgu_b[:, :L])[None, None, :, :, None]
    return u * s


def conv_ffn(h, conv_prefix, ffn_w_up, ffn_conv_w, ffn_conv_b, ffn_w_down):
    T = h.shape[1]
    up = jnp.einsum('btd,df->btf', h, ffn_w_up)
    a, b = jnp.split(up, 2, axis=-1)
    a_ext = jnp.concatenate([conv_prefix.astype(a.dtype), a], axis=1)
    conv = sum(a_ext[:, j:j + T] * ffn_conv_w[j] for j in range(CONV_W)) + ffn_conv_b
    out = jnp.einsum('btf,fd->btd', jax.nn.gelu(conv) * b, ffn_w_down)
    return out, a_ext[:, -(CONV_W - 1):]


def layer_forward(x, c, win_k, win_v, pool_prefix, conv_prefix, pos0,
                  norm1_g, norm2_g, w_ada, b_ada, w_in, attn_sink, w_o_attn,
                  pool_w, pool_scale, w_o_pool, sgu_norm_g, sgu_w, sgu_b, w_o_sgu,
                  w_out, ffn_w_up, ffn_conv_w, ffn_conv_b, ffn_w_down):
    B, T, _ = x.shape
    mod = jnp.einsum('bd,de->be', jax.nn.silu(c), w_ada) + b_ada
    sh1, sc1, g1, sh2, sc2, g2 = [m[:, None, :] for m in jnp.split(mod, 6, axis=-1)]
    h = rmsnorm(x, norm1_g) * (1 + sc1) + sh1
    z = jnp.einsum('btd,de->bte', h, w_in)
    q, k, v, pu, su, sv, gates = jnp.split(z, IN_SPLITS, axis=-1)
    q = q.reshape(B, T, N_KV_HEADS, GROUP, HEAD_DIM)
    k = k.reshape(B, T, N_KV_HEADS, HEAD_DIM)
    v = v.reshape(B, T, N_KV_HEADS, HEAD_DIM)
    if win_k is None:
        ya = attn_prompt(q, k, v, attn_sink)
        keep = min(WINDOW, T)
        new_k, new_v = k[:, T - keep:], v[:, T - keep:]
    else:
        ya, new_k, new_v = attn_sample(q, k, v, win_k, win_v, pos0, attn_sink)
    u_ext = jnp.concatenate([pool_prefix.astype(pu.dtype), pu], axis=1)
    yb = pool_mixer(u_ext, pos0, pool_w, pool_scale)
    new_pool = u_ext[:, -POOL_STATE:]
    uu = jax.nn.gelu(su)
    vn = rmsnorm(jax.nn.gelu(sv), sgu_norm_g)
    L = SGU_LEN if win_k is None else T
    yc = spatial_gate(uu.reshape(B, T // L, L, SGU_GROUPS, SGU_GROUP_WIDTH),
                      vn.reshape(B, T // L, L, SGU_GROUPS, SGU_GROUP_WIDTH),
                      sgu_w, sgu_b).reshape(B, T, SGU_WIDTH)
    ga, gb, gc = jnp.split(gates, N_BRANCH, axis=-1)
    merged = (jax.nn.sigmoid(ga) * jnp.einsum('bte,ed->btd', ya, w_o_attn)
              + jax.nn.sigmoid(gb) * jnp.einsum('bte,ed->btd', yb, w_o_pool)
              + jax.nn.sigmoid(gc) * jnp.einsum('bte,ed->btd', yc, w_o_sgu))
    x = x + g1 * jnp.einsum('btd,de->bte', merged, w_out)
    h2 = rmsnorm(x, norm2_g) * (1 + sc2) + sh2
    f, new_conv = conv_ffn(h2, conv_prefix, ffn_w_up, ffn_conv_w, ffn_conv_b, ffn_w_down)
    x = x + g2 * f
    return x, new_k, new_v, new_pool, new_conv, vn


def setup_inputs(seed: int = 0) -> dict:
    key = jax.random.key(seed)
    ks = jax.random.split(key, 28)

    def nrm(k, shape, scale):
        return jax.random.normal(k, shape, jnp.float32) * scale

    win_cache = min(WINDOW, PAST_LEN)
    return {
        "x_prompt": nrm(ks[0], (BATCH, SEQ, D_MODEL), 1.0),
        "x_sample": nrm(ks[1], (DEC_BATCH, DEC_SEQ, D_MODEL), 1.0),
        "c_prompt": nrm(ks[2], (BATCH, D_MODEL), 1.0),
        "c_sample": nrm(ks[3], (DEC_BATCH, D_MODEL), 1.0),
        "cache_k_win": nrm(ks[4], (DEPTH, DEC_BATCH, win_cache, N_KV_HEADS, HEAD_DIM), 1.0),
        "cache_v_win": nrm(ks[5], (DEPTH, DEC_BATCH, win_cache, N_KV_HEADS, HEAD_DIM), 1.0),
        "state_pool": nrm(ks[6], (DEPTH, DEC_BATCH, POOL_STATE, POOL_WIDTH), 1.0),
        "state_ffn_conv": nrm(ks[7], (DEPTH, DEC_BATCH, CONV_W - 1, D_FF), 1.0),
        "norm1_g": 1.0 + nrm(ks[8], (DEPTH, D_MODEL), 0.05),
        "norm2_g": 1.0 + nrm(ks[9], (DEPTH, D_MODEL), 0.05),
        "w_ada": nrm(ks[10], (DEPTH, D_MODEL, 6 * D_MODEL), 0.5 * D_MODEL ** -0.5),
        "b_ada": nrm(ks[11], (DEPTH, 6 * D_MODEL), 0.02),
        "w_in": nrm(ks[12], (DEPTH, D_MODEL, N_IN), D_MODEL ** -0.5),
        "attn_sink": nrm(ks[13], (DEPTH, N_HEADS), 0.5),
        "w_o_attn": nrm(ks[14], (DEPTH, ATTN_WIDTH, D_MODEL), ATTN_WIDTH ** -0.5),
        "pool_w": nrm(ks[15], (DEPTH, POOL_GROUPS, POOL_GROUP_WIDTH, POOL_GROUP_WIDTH), POOL_GROUP_WIDTH ** -0.5),
        "pool_scale": 1.0 + nrm(ks[16], (DEPTH, POOL_WIDTH), 0.1),
        "w_o_pool": nrm(ks[17], (DEPTH, POOL_WIDTH, D_MODEL), POOL_WIDTH ** -0.5),
        "sgu_norm_g": 1.0 + nrm(ks[18], (DEPTH, SGU_WIDTH), 0.05),
        "sgu_w": nrm(ks[19], (DEPTH, SGU_GROUPS, SGU_LEN, SGU_LEN), SGU_LEN ** -0.5),
        "sgu_b": 1.0 + nrm(ks[20], (DEPTH, SGU_GROUPS, SGU_LEN), 0.1),
        "w_o_sgu": nrm(ks[21], (DEPTH, SGU_WIDTH, D_MODEL), SGU_WIDTH ** -0.5),
        "w_out": nrm(ks[22], (DEPTH, D_MODEL, D_MODEL), D_MODEL ** -0.5),
        "ffn_w_up": nrm(ks[23], (DEPTH, D_MODEL, 2 * D_FF), D_MODEL ** -0.5),
        "ffn_conv_w": nrm(ks[24], (DEPTH, CONV_W, D_FF), CONV_W ** -0.5),
        "ffn_conv_b": nrm(ks[25], (DEPTH, D_FF), 0.02),
        "ffn_w_down": nrm(ks[26], (DEPTH, D_FF, D_MODEL), D_FF ** -0.5),
        "final_norm_g": 1.0 + nrm(ks[27], (D_MODEL,), 0.05),
    }


def reference(x_prompt, x_sample, c_prompt, c_sample, cache_k_win, cache_v_win, state_pool, state_ffn_conv,
              norm1_g, norm2_g, w_ada, b_ada, w_in, attn_sink, w_o_attn, pool_w, pool_scale, w_o_pool,
              sgu_norm_g, sgu_w, sgu_b, w_o_sgu, w_out, ffn_w_up, ffn_conv_w, ffn_conv_b, ffn_w_down,
              final_norm_g):
    layer_params = (norm1_g, norm2_g, w_ada, b_ada, w_in, attn_sink, w_o_attn, pool_w, pool_scale, w_o_pool,
                    sgu_norm_g, sgu_w, sgu_b, w_o_sgu, w_out, ffn_w_up, ffn_conv_w, ffn_conv_b, ffn_w_down)
    xp, xs = x_prompt, x_sample
    kp, vp, pp, cp = [], [], [], []
    ks_, vs_, ps_, cs_, ss_ = [], [], [], [], []
    for l in range(DEPTH):
        lw = [p[l] for p in layer_params]
        zero_pool = jnp.zeros((xp.shape[0], POOL_STATE, POOL_WIDTH), xp.dtype)
        zero_conv = jnp.zeros((xp.shape[0], CONV_W - 1, D_FF), xp.dtype)
        xp, nk, nv, npool, nconv, _ = layer_forward(xp, c_prompt, None, None, zero_pool, zero_conv, 0, *lw)
        kp.append(nk); vp.append(nv); pp.append(npool); cp.append(nconv)
        xs, nk, nv, npool, nconv, nsv = layer_forward(xs, c_sample, cache_k_win[l], cache_v_win[l],
                                                     state_pool[l], state_ffn_conv[l], PAST_LEN, *lw)
        ks_.append(nk); vs_.append(nv); ps_.append(npool); cs_.append(nconv); ss_.append(nsv)
    y_prompt = rmsnorm(xp, final_norm_g)
    y_sample = rmsnorm(xs, final_norm_g)
    return (y_prompt, y_sample,
            jnp.stack(kp), jnp.stack(vp), jnp.stack(pp), jnp.stack(cp),
            jnp.stack(ks_), jnp.stack(vs_), jnp.stack(ps_), jnp.stack(cs_), jnp.stack(ss_))
```

```python
import functools

import jax
import jax.numpy as jnp
from jax import lax
from jax.experimental import pallas as pl
from jax.experimental.pallas import tpu as pltpu

F32 = jnp.float32
BF16 = jnp.bfloat16

D_MODEL = 1024
CHUNK = 64
EPS = 1e-6
N_HEADS = 16
N_KV_HEADS = 2
GROUP = N_HEADS // N_KV_HEADS
HEAD_DIM = 64
WINDOW = 128
WINDOW_CHUNKS = WINDOW // CHUNK
ATTN_WIDTH = N_HEADS * HEAD_DIM
KV_WIDTH = N_KV_HEADS * HEAD_DIM
POOL_WINDOWS = (2, 4, 8, 16)
POOL_GROUPS = 4
POOL_GROUP_WIDTH = D_MODEL // 8
POOL_WIDTH = POOL_GROUPS * POOL_GROUP_WIDTH
POOL_STATE = 16 - 1
SGU_LEN = 128
SGU_GROUPS = 4
SGU_GROUP_WIDTH = D_MODEL // 8
SGU_WIDTH = SGU_GROUPS * SGU_GROUP_WIDTH
N_BRANCH = 3
Q_END = ATTN_WIDTH
K_END = Q_END + KV_WIDTH
V_END = K_END + KV_WIDTH
POOL_END = V_END + POOL_WIDTH
SU_END = POOL_END + SGU_WIDTH
SV_END = SU_END + SGU_WIDTH
N_IN = SV_END + N_BRANCH * D_MODEL
D_FF = 2816
CONV_W = 3
PAST_LEN = 1024

LANES = 128
SUBLANES = 8
HALF_LANES = LANES // 2
VMEM_LIMIT_BYTES = 56 * 1024 * 1024

NEG_BIG = -1e30
PAIRS = GROUP // 2
KEY_WIN = 4 * CHUNK
PREV_ROWS = KEY_WIN - CHUNK
POOL_PAD = 16
CONV_PAD = SUBLANES
MOD_ROWS = 16


def _alibi_slope(head):
    return 2.0 ** (-8.0 * (head + 1) / N_HEADS)


def _rmsnorm(x, g):
    return x * lax.rsqrt(jnp.mean(x * x, axis=-1, keepdims=True) + EPS) * g


def _dot(a, b):
    return jnp.dot(a, b, preferred_element_type=F32)


def _dot_nt(a, b):
    return lax.dot_general(a, b, (((1,), (1,)), ((), ())), preferred_element_type=F32)


def _mod_kernel(c_ref, w_ref, b_ref, o_ref):
    c = c_ref[...]
    sc = c * jax.nn.sigmoid(c)
    o_ref[0] = jnp.dot(sc, w_ref[0], preferred_element_type=F32,
                       precision=lax.Precision.HIGHEST) + b_ref[0]


def _modulation(c_all, w_ada, b_ada):
    depth, d, n = w_ada.shape
    bn = n // 4
    return pl.pallas_call(
        _mod_kernel,
        grid=(depth, n // bn),
        in_specs=[
            pl.BlockSpec((MOD_ROWS, d), lambda l, j: (0, 0)),
            pl.BlockSpec((1, d, bn), lambda l, j: (l, 0, j)),
            pl.BlockSpec((1, 1, bn), lambda l, j: (l, 0, j)),
        ],
        out_specs=pl.BlockSpec((1, MOD_ROWS, bn), lambda l, j: (l, 0, j)),
        out_shape=jax.ShapeDtypeStruct((depth, MOD_ROWS, n), F32),
        compiler_params=pltpu.CompilerParams(
            dimension_semantics=("arbitrary", "arbitrary"),
            vmem_limit_bytes=VMEM_LIMIT_BYTES),
        name="adaln_modulation",
    )(c_all, w_ada, b_ada.reshape(depth, 1, n))


def _split_heads(z):
    lane = lax.broadcasted_iota(jnp.int32, z.shape, 1)
    low = lane < HALF_LANES
    zs = pltpu.roll(z, HALF_LANES, axis=1)
    zero = jnp.zeros_like(z)
    top0 = jnp.where(low, z, zero)
    bot0 = jnp.where(low, zero, zs)
    top1 = jnp.where(low, zs, zero)
    bot1 = jnp.where(low, zero, z)
    return top0, bot0, top1, bot1


def _attend(qs, kt, vb, bias, sink_a, sink_b):
    w = kt.shape[0] // 2
    logits = _dot_nt(qs, kt) + bias
    la = logits[:, :w]
    lb = logits[:, w:]
    ma = jnp.maximum(jnp.max(la, axis=-1, keepdims=True), sink_a)
    mb = jnp.maximum(jnp.max(lb, axis=-1, keepdims=True), sink_b)
    pa = jnp.exp(la - ma)
    pb = jnp.exp(lb - mb)
    da = jnp.sum(pa, axis=-1, keepdims=True) + jnp.exp(sink_a - ma)
    db = jnp.sum(pb, axis=-1, keepdims=True) + jnp.exp(sink_b - mb)
    p = jnp.concatenate([pa, pb], axis=1).astype(BF16)
    o = _dot(p, vb)
    lane = lax.broadcasted_iota(jnp.int32, o.shape, 1)
    inv = jnp.where(lane < HALF_LANES, 1.0 / da, 1.0 / db)
    return o * inv


def _pool_window_sums(e):
    outs = []
    for gi, w in enumerate(POOL_WINDOWS):
        s = e[:, gi * POOL_GROUP_WIDTH:(gi + 1) * POOL_GROUP_WIDTH]
        k = 1
        while k < w:
            s = s + pltpu.roll(s, k, axis=0)
            k *= 2
        outs.append(s)
    return outs


def _pool_branch(sums, pu, pos, poolw_ref, pscale):
    outs = []
    for gi, w in enumerate(POOL_WINDOWS):
        c0, c1 = gi * POOL_GROUP_WIDTH, (gi + 1) * POOL_GROUP_WIDTH
        cnt = jnp.minimum(pos + 1, w).astype(F32)
        d = sums[gi] / cnt - pu[:, c0:c1]
        outs.append(_dot(d.astype(BF16), poolw_ref[gi]))
    return jnp.concatenate(outs, axis=1) * pscale


def _sgu_weight(sguw_ref, g, length):
    i = lax.broadcasted_iota(jnp.int32, (length, length), 0)
    j = lax.broadcasted_iota(jnp.int32, (length, length), 1)
    w = sguw_ref[g, :length, :length]
    return jnp.where((j // CHUNK) <= (i // CHUNK), w, jnp.zeros_like(w))


def _merge_and_project(x, g1, hb, ya, yb, yc, win_ref, woa_ref, wop_ref, wos_ref, wout_ref):
    zg = _dot(hb, win_ref[:, SV_END:N_IN])
    ga = zg[:, :D_MODEL]
    gb = zg[:, D_MODEL:2 * D_MODEL]
    gc = zg[:, 2 * D_MODEL:]
    merged = (jax.nn.sigmoid(ga) * _dot(ya, woa_ref[...])
              + jax.nn.sigmoid(gb) * _dot(yb.astype(BF16), wop_ref[...])
              + jax.nn.sigmoid(gc) * _dot(yc.astype(BF16), wos_ref[...]))
    return x + g1 * _dot(merged.astype(BF16), wout_ref[...])


def _build_prompt_bias(bias_tab):
    q = lax.broadcasted_iota(jnp.int32, (CHUNK, KEY_WIN), 0)
    j = lax.broadcasted_iota(jnp.int32, (CHUNK, KEY_WIN), 1)
    dist = jnp.abs(PREV_ROWS + q - j).astype(F32)
    for variant, first_valid in enumerate((3 * CHUNK, 2 * CHUNK, CHUNK)):
        valid = j >= first_valid
        for kv in range(N_KV_HEADS):
            for p in range(PAIRS):
                for par in range(2):
                    slope = _alibi_slope(kv * GROUP + 2 * p + par)
                    blk = jnp.where(valid, -slope * dist, NEG_BIG)
                    bias_tab[variant, kv, p * CHUNK:(p + 1) * CHUNK,
                             par * KEY_WIN:(par + 1) * KEY_WIN] = blk


def _tokmix_prompt_kernel(x_ref, mod_ref, n1g_ref, win_ref, woa_ref, wop_ref, wos_ref, wout_ref,
                          poolw_ref, pscale_ref, sgug_ref, sguw_ref, sgub_ref, sink_ref,
                          xo_ref, nk_ref, nv_ref, npool_ref,
                          kext, vext, pext, ya_scr, bias_tab, *, tt):
    b = pl.program_id(0)
    t = pl.program_id(1)
    n_chunks = tt // CHUNK

    @pl.when(jnp.logical_and(b == 0, t == 0))
    def _():
        _build_prompt_bias(bias_tab)

    @pl.when(t == 0)
    def _():
        kext[:, 0:PREV_ROWS, :] = jnp.zeros((4, PREV_ROWS, LANES), BF16)
        vext[:, 0:PREV_ROWS, :] = jnp.zeros((4, PREV_ROWS, LANES), BF16)
        pext[0:POOL_PAD, :] = jnp.zeros((POOL_PAD, POOL_WIDTH), F32)

    x = x_ref[0]
    sh1 = mod_ref[0, :, 0:D_MODEL]
    sc1 = mod_ref[0, :, D_MODEL:2 * D_MODEL]
    g1 = mod_ref[0, :, 2 * D_MODEL:3 * D_MODEL]
    h = _rmsnorm(x, n1g_ref[...]) * (1.0 + sc1) + sh1
    hb = h.astype(BF16)

    zqkv = _dot(hb, win_ref[:, 0:V_END])
    zk = zqkv[:, Q_END:K_END]
    zv = zqkv[:, K_END:V_END]
    nk_ref[0] = zk[tt - WINDOW:, :]
    nv_ref[0] = zv[tt - WINDOW:, :]
    qb = (zqkv[:, 0:Q_END] * (HEAD_DIM ** -0.5)).astype(BF16)
    for idx, part in enumerate(_split_heads(zk)):
        kext[idx, PREV_ROWS:PREV_ROWS + tt, :] = part.astype(BF16)
    for idx, part in enumerate(_split_heads(zv)):
        vext[idx, PREV_ROWS:PREV_ROWS + tt, :] = part.astype(BF16)

    for i in range(n_chunks):
        r0 = i * CHUNK
        variant = jnp.where(t == 0, min(i, 2), 2) if i < 2 else 2
        for kv in range(N_KV_HEADS):
            qs = jnp.concatenate(
                [qb[r0:r0 + CHUNK, (kv * PAIRS + p) * LANES:(kv * PAIRS + p + 1) * LANES]
                 for p in range(PAIRS)], axis=0)
            kt = jnp.concatenate([kext[2 * kv, r0:r0 + KEY_WIN, :],
                                  kext[2 * kv + 1, r0:r0 + KEY_WIN, :]], axis=0)
            vb = jnp.concatenate([vext[2 * kv, r0:r0 + KEY_WIN, :],
                                  vext[2 * kv + 1, r0:r0 + KEY_WIN, :]], axis=0)
            o = _attend(qs, kt, vb, bias_tab[variant, kv], sink_ref[kv, 0], sink_ref[kv, 1])
            for p in range(PAIRS):
                ya_scr[r0:r0 + CHUNK, (kv * PAIRS + p) * LANES:(kv * PAIRS + p + 1) * LANES] = (
                    o[p * CHUNK:(p + 1) * CHUNK, :].astype(BF16))

    kext[:, 0:PREV_ROWS, :] = kext[:, tt:tt + PREV_ROWS, :]
    vext[:, 0:PREV_ROWS, :] = vext[:, tt:tt + PREV_ROWS, :]

    zmix = _dot(hb, win_ref[:, V_END:SV_END])
    pu = zmix[:, 0:POOL_WIDTH]
    npool_ref[0] = pu[tt - POOL_STATE:, :]
    pext[POOL_PAD:POOL_PAD + tt, :] = pu
    sums = [s[POOL_PAD:, :] for s in _pool_window_sums(pext[...])]
    pos = t * tt + lax.broadcasted_iota(jnp.int32, (tt, POOL_GROUP_WIDTH), 0)
    yb = _pool_branch(sums, pu, pos, poolw_ref, pscale_ref[...])
    pext[0:POOL_PAD, :] = pext[tt:tt + POOL_PAD, :]

    uu = jax.nn.gelu(zmix[:, POOL_WIDTH:POOL_WIDTH + SGU_WIDTH])
    vn = _rmsnorm(jax.nn.gelu(zmix[:, POOL_WIDTH + SGU_WIDTH:]), sgug_ref[...])
    vnb = vn.astype(BF16)
    n_sgu = tt // SGU_LEN
    yc_cols = []
    for g in range(SGU_GROUPS):
        c0, c1 = g * SGU_GROUP_WIDTH, (g + 1) * SGU_GROUP_WIDTH
        rhs = jnp.concatenate([vnb[n * SGU_LEN:(n + 1) * SGU_LEN, c0:c1] for n in range(n_sgu)], axis=1)
        s = _dot(_sgu_weight(sguw_ref, g, SGU_LEN), rhs) + sgub_ref[:, g:g + 1]
        s_rows = jnp.concatenate([s[:, n * SGU_GROUP_WIDTH:(n + 1) * SGU_GROUP_WIDTH]
                                  for n in range(n_sgu)], axis=0)
        yc_cols.append(uu[:, c0:c1] * s_rows)
    yc = jnp.concatenate(yc_cols, axis=1)

    xo_ref[0] = _merge_and_project(x, g1, hb, ya_scr[...], yb, yc,
                                   win_ref, woa_ref, wop_ref, wos_ref, wout_ref)


def _resident(shape):
    return pl.BlockSpec(shape, lambda *_: (0,) * len(shape), pipeline_mode=pl.Buffered(1))


def _tokmix_prompt(x, mod, lw, tt):
    bsz, seq, d = x.shape
    nt = seq // tt
    rows = PAIRS * CHUNK
    kernel = functools.partial(_tokmix_prompt_kernel, tt=tt)
    out_shape = (
        jax.ShapeDtypeStruct((bsz, seq, d), F32),
        jax.ShapeDtypeStruct((bsz, WINDOW, KV_WIDTH), F32),
        jax.ShapeDtypeStruct((bsz, WINDOW, KV_WIDTH), F32),
        jax.ShapeDtypeStruct((bsz, POOL_STATE, POOL_WIDTH), F32),
    )
    return pl.pallas_call(
        kernel,
        grid=(bsz, nt),
        in_specs=[
            pl.BlockSpec((1, tt, d), lambda b, t: (b, t, 0)),
            pl.BlockSpec((1, 1, 6 * d), lambda b, t: (b, 0, 0)),
            _resident((1, d)),
            _resident((d, N_IN)),
            _resident((ATTN_WIDTH, d)),
            _resident((POOL_WIDTH, d)),
            _resident((SGU_WIDTH, d)),
            _resident((d, d)),
            _resident((POOL_GROUPS, POOL_GROUP_WIDTH, POOL_GROUP_WIDTH)),
            _resident((1, POOL_WIDTH)),
            _resident((1, SGU_WIDTH)),
            _resident((SGU_GROUPS, SGU_LEN, SGU_LEN)),
            _resident((SGU_LEN, SGU_GROUPS)),
            _resident((N_KV_HEADS, 2, rows, 1)),
        ],
        out_specs=(
            pl.BlockSpec((1, tt, d), lambda b, t: (b, t, 0)),
            pl.BlockSpec((1, WINDOW, KV_WIDTH), lambda b, t: (b, 0, 0)),
            pl.BlockSpec((1, WINDOW, KV_WIDTH), lambda b, t: (b, 0, 0)),
            pl.BlockSpec((1, POOL_STATE, POOL_WIDTH), lambda b, t: (b, 0, 0)),
        ),
        out_shape=out_shape,
        scratch_shapes=[
            pltpu.VMEM((4, PREV_ROWS + tt, LANES), BF16),
            pltpu.VMEM((4, PREV_ROWS + tt, LANES), BF16),
            pltpu.VMEM((POOL_PAD + tt, POOL_WIDTH), F32),
            pltpu.VMEM((tt, ATTN_WIDTH), BF16),
            pltpu.VMEM((3, N_KV_HEADS, rows, 2 * KEY_WIN), F32),
        ],
        compiler_params=pltpu.CompilerParams(
            dimension_semantics=("arbitrary", "arbitrary"),
            vmem_limit_bytes=VMEM_LIMIT_BYTES),
        name="tokmix_prompt",
    )(x, mod, lw["norm1_g"], lw["w_in"], lw["w_o_attn"], lw["w_o_pool"], lw["w_o_sgu"], lw["w_out"],
      lw["pool_w"], lw["pool_scale"], lw["sgu_norm_g"], lw["sgu_w"], lw["sgu_b_t"],
      _sink_table(lw["attn_sink"], CHUNK))


def _sink_table(sink, rows_per_pair):
    s = sink.astype(F32).reshape(N_KV_HEADS, PAIRS, 2)
    s = jnp.transpose(s, (0, 2, 1))
    s = jnp.repeat(s, rows_per_pair, axis=2)
    return s[..., None]


def _conv_ffn_tail(x, g2, a, a1, a2, bgate, cw_ref, cb_ref, wdown_ref):
    conv = a2 * cw_ref[0:1, :] + a1 * cw_ref[1:2, :] + a * cw_ref[2:3, :] + cb_ref[...]
    f = _dot((jax.nn.gelu(conv) * bgate).astype(BF16), wdown_ref[...])
    return x + g2 * f


def _channel_prompt_kernel(x_ref, mod_ref, n2g_ref, wup_ref, cw_ref, cb_ref, wdown_ref, fng_ref,
                           xo_ref, nconv_ref, aext, *, tt, final):
    t = pl.program_id(1)

    @pl.when(t == 0)
    def _():
        aext[0:CONV_PAD, :] = jnp.zeros((CONV_PAD, D_FF), F32)

    x = x_ref[0]
    sh2 = mod_ref[0, :, 3 * D_MODEL:4 * D_MODEL]
    sc2 = mod_ref[0, :, 4 * D_MODEL:5 * D_MODEL]
    g2 = mod_ref[0, :, 5 * D_MODEL:6 * D_MODEL]
    h = _rmsnorm(x, n2g_ref[...]) * (1.0 + sc2) + sh2
    up = _dot(h.astype(BF16), wup_ref[...])
    a = up[:, 0:D_FF]
    bgate = up[:, D_FF:]
    nconv_ref[0] = a[tt - (CONV_W - 1):, :]
    aext[CONV_PAD:CONV_PAD + tt, :] = a
    a1 = aext[CONV_PAD - 1:CONV_PAD - 1 + tt, :]
    a2 = aext[CONV_PAD - 2:CONV_PAD - 2 + tt, :]
    y = _conv_ffn_tail(x, g2, a, a1, a2, bgate, cw_ref, cb_ref, wdown_ref)
    aext[0:CONV_PAD, :] = aext[tt:tt + CONV_PAD, :]
    if final:
        y = _rmsnorm(y, fng_ref[...])
    xo_ref[0] = y


def _channel_prompt(x, mod, lw, final_g, tt, final):
    bsz, seq, d = x.shape
    nt = seq // tt
    kernel = functools.partial(_channel_prompt_kernel, tt=tt, final=final)
    return pl.pallas_call(
        kernel,
        grid=(bsz, nt),
        in_specs=[
            pl.BlockSpec((1, tt, d), lambda b, t: (b, t, 0)),
            pl.BlockSpec((1, 1, 6 * d), lambda b, t: (b, 0, 0)),
            _resident((1, d)),
            _resident((d, 2 * D_FF)),
            _resident((CONV_W, D_FF)),
            _resident((1, D_FF)),
            _resident((D_FF, d)),
            _resident((1, d)),
        ],
        out_specs=(
            pl.BlockSpec((1, tt, d), lambda b, t: (b, t, 0)),
            pl.BlockSpec((1, CONV_W - 1, D_FF), lambda b, t: (b, 0, 0)),
        ),
        out_shape=(
            jax.ShapeDtypeStruct((bsz, seq, d), F32),
            jax.ShapeDtypeStruct((bsz, CONV_W - 1, D_FF), F32),
        ),
        scratch_shapes=[pltpu.VMEM((CONV_PAD + tt, D_FF), F32)],
        compiler_params=pltpu.CompilerParams(
            dimension_semantics=("arbitrary", "arbitrary"),
            vmem_limit_bytes=VMEM_LIMIT_BYTES),
        name="channel_prompt",
    )(x, mod, lw["norm2_g"], lw["ffn_w_up"], lw["ffn_conv_w"], lw["ffn_conv_b"], lw["ffn_w_down"],
      final_g)


def _tokmix_sample_kernel(x_ref, mod_ref, ck_ref, cv_ref, sp_ref, n1g_ref, win_ref, woa_ref, wop_ref,
                          wos_ref, wout_ref, poolw_ref, pscale_ref, sgug_ref, sguw_ref, sgub_ref,
                          sink_ref,
                          xo_ref, nk_ref, nv_ref, npool_ref, vn_ref,
                          ya_scr, *, pos0):
    nb, ts, d = x_ref.shape
    wc = ck_ref.shape[1]
    n_keys = wc + ts
    key_win = 2 * LANES

    x3 = x_ref[...]
    sh1 = mod_ref[:, :, 0:D_MODEL]
    sc1 = mod_ref[:, :, D_MODEL:2 * D_MODEL]
    g1 = mod_ref[:, :, 2 * D_MODEL:3 * D_MODEL]
    h3 = _rmsnorm(x3, n1g_ref[...]) * (1.0 + sc1) + sh1
    hb = h3.reshape(nb * ts, d).astype(BF16)

    zqkv = _dot(hb, win_ref[:, 0:V_END])
    qb = (zqkv[:, 0:Q_END] * (HEAD_DIM ** -0.5)).astype(BF16)

    q = lax.broadcasted_iota(jnp.int32, (ts, key_win), 0)
    j = lax.broadcasted_iota(jnp.int32, (ts, key_win), 1)
    dist = jnp.abs(wc + q - j).astype(F32)
    valid = jnp.logical_and(j < n_keys, (pos0 - wc + j) >= 0)
    bias = []
    for kv in range(N_KV_HEADS):
        blocks = []
        for par in range(2):
            col = [jnp.where(valid, -_alibi_slope(kv * GROUP + 2 * p + par) * dist, NEG_BIG)
                   for p in range(PAIRS)]
            blocks.append(jnp.concatenate(col, axis=0))
        bias.append(jnp.concatenate(blocks, axis=1))

    pad = jnp.zeros((key_win - n_keys, KV_WIDTH), F32)
    for b in range(nb):
        r0 = b * ts
        zk = zqkv[r0:r0 + ts, Q_END:K_END]
        zv = zqkv[r0:r0 + ts, K_END:V_END]
        ck = ck_ref[b]
        cv = cv_ref[b]
        nk_ref[b] = jnp.concatenate([ck[ts:, :], zk], axis=0)
        nv_ref[b] = jnp.concatenate([cv[ts:, :], zv], axis=0)
        ksplit = [part.astype(BF16) for part in _split_heads(jnp.concatenate([ck, zk, pad], axis=0))]
        vsplit = [part.astype(BF16) for part in _split_heads(jnp.concatenate([cv, zv, pad], axis=0))]
        for kv in range(N_KV_HEADS):
            qs = jnp.concatenate(
                [qb[r0:r0 + ts, (kv * PAIRS + p) * LANES:(kv * PAIRS + p + 1) * LANES]
                 for p in range(PAIRS)], axis=0)
            kt = jnp.concatenate([ksplit[2 * kv], ksplit[2 * kv + 1]], axis=0)
            vb = jnp.concatenate([vsplit[2 * kv], vsplit[2 * kv + 1]], axis=0)
            o = _attend(qs, kt, vb, bias[kv], sink_ref[kv, 0], sink_ref[kv, 1])
            for p in range(PAIRS):
                ya_scr[r0:r0 + ts, (kv * PAIRS + p) * LANES:(kv * PAIRS + p + 1) * LANES] = (
                    o[p * ts:(p + 1) * ts, :].astype(BF16))

    zmix = _dot(hb, win_ref[:, V_END:SV_END])
    pu = zmix[:, 0:POOL_WIDTH]
    head = jnp.zeros((POOL_PAD - POOL_STATE, POOL_WIDTH), F32)
    sums = [[] for _ in POOL_WINDOWS]
    for b in range(nb):
        pub = pu[b * ts:(b + 1) * ts, :]
        npool_ref[b] = pub[ts - POOL_STATE:, :]
        e = jnp.concatenate([head, sp_ref[b], pub], axis=0)
        for gi, s in enumerate(_pool_window_sums(e)):
            sums[gi].append(s[POOL_PAD:, :])
    sums = [jnp.concatenate(s, axis=0) for s in sums]
    pos = pos0 + lax.broadcasted_iota(jnp.int32, (nb, ts, POOL_GROUP_WIDTH), 1).reshape(
        nb * ts, POOL_GROUP_WIDTH)
    yb = _pool_branch(sums, pu, pos, poolw_ref, pscale_ref[...])

    uu = jax.nn.gelu(zmix[:, POOL_WIDTH:POOL_WIDTH + SGU_WIDTH])
    vn = _rmsnorm(jax.nn.gelu(zmix[:, POOL_WIDTH + SGU_WIDTH:]), sgug_ref[...])
    vn_ref[...] = vn.reshape(nb, ts, SGU_WIDTH)
    vnb = vn.astype(BF16)
    yc_cols = []
    for g in range(SGU_GROUPS):
        c0, c1 = g * SGU_GROUP_WIDTH, (g + 1) * SGU_GROUP_WIDTH
        rhs = jnp.concatenate([vnb[b * ts:(b + 1) * ts, c0:c1] for b in range(nb)], axis=1)
        s = _dot(_sgu_weight(sguw_ref, g, ts), rhs) + sgub_ref[0:ts, g:g + 1]
        s_rows = jnp.concatenate([s[:, b * SGU_GROUP_WIDTH:(b + 1) * SGU_GROUP_WIDTH]
                                  for b in range(nb)], axis=0)
        yc_cols.append(uu[:, c0:c1] * s_rows)
    yc = jnp.concatenate(yc_cols, axis=1)

    x2 = x3.reshape(nb * ts, d)
    g1r = jnp.broadcast_to(g1, (nb, ts, d)).reshape(nb * ts, d)
    y = _merge_and_project(x2, g1r, hb, ya_scr[...], yb, yc,
                           win_ref, woa_ref, wop_ref, wos_ref, wout_ref)
    xo_ref[...] = y.reshape(nb, ts, d)


def _full(shape):
    return pl.BlockSpec(shape, lambda *_: (0,) * len(shape))


def _tokmix_sample(x, mod, cache_k, cache_v, state_pool, lw, pos0):
    nb, ts, d = x.shape
    wc = cache_k.shape[1]
    kernel = functools.partial(_tokmix_sample_kernel, pos0=pos0)
    args = (x, mod, cache_k, cache_v, state_pool, lw["norm1_g"], lw["w_in"], lw["w_o_attn"],
            lw["w_o_pool"], lw["w_o_sgu"], lw["w_out"], lw["pool_w"], lw["pool_scale"],
            lw["sgu_norm_g"], lw["sgu_w"], lw["sgu_b_t"], _sink_table(lw["attn_sink"], ts))
    out_shape = (
        jax.ShapeDtypeStruct((nb, ts, d), F32),
        jax.ShapeDtypeStruct((nb, wc, KV_WIDTH), F32),
        jax.ShapeDtypeStruct((nb, wc, KV_WIDTH), F32),
        jax.ShapeDtypeStruct((nb, POOL_STATE, POOL_WIDTH), F32),
        jax.ShapeDtypeStruct((nb, ts, SGU_WIDTH), F32),
    )
    return pl.pallas_call(
        kernel,
        grid=(1,),
        in_specs=[_full(a.shape) for a in args],
        out_specs=tuple(_full(s.shape) for s in out_shape),
        out_shape=out_shape,
        scratch_shapes=[pltpu.VMEM((nb * ts, ATTN_WIDTH), BF16)],
        compiler_params=pltpu.CompilerParams(
            dimension_semantics=("arbitrary",),
            vmem_limit_bytes=VMEM_LIMIT_BYTES),
        name="tokmix_sample",
    )(*args)


def _channel_sample_kernel(x_ref, mod_ref, sc_ref, n2g_ref, wup_ref, cw_ref, cb_ref, wdown_ref, fng_ref,
                           xo_ref, nconv_ref, aext, *, final):
    nb, ts, d = x_ref.shape
    x3 = x_ref[...]
    sh2 = mod_ref[:, :, 3 * D_MODEL:4 * D_MODEL]
    sc2 = mod_ref[:, :, 4 * D_MODEL:5 * D_MODEL]
    g2 = mod_ref[:, :, 5 * D_MODEL:6 * D_MODEL]
    h3 = _rmsnorm(x3, n2g_ref[...]) * (1.0 + sc2) + sh2
    up = _dot(h3.reshape(nb * ts, d).astype(BF16), wup_ref[...])
    a = up[:, 0:D_FF]
    bgate = up[:, D_FF:]
    a1s, a2s = [], []
    for b in range(nb):
        ab = a[b * ts:(b + 1) * ts, :]
        nconv_ref[b] = ab[ts - (CONV_W - 1):, :]
        aext[b, CONV_PAD - (CONV_W - 1):CONV_PAD, :] = sc_ref[b]
        aext[b, CONV_PAD:CONV_PAD + ts, :] = ab
        a1s.append(aext[b, CONV_PAD - 1:CONV_PAD - 1 + ts, :])
        a2s.append(aext[b, CONV_PAD - 2:CONV_PAD - 2 + ts, :])
    a1 = jnp.concatenate(a1s, axis=0)
    a2 = jnp.concatenate(a2s, axis=0)
    x2 = x3.reshape(nb * ts, d)
    g2r = jnp.broadcast_to(g2, (nb, ts, d)).reshape(nb * ts, d)
    y = _conv_ffn_tail(x2, g2r, a, a1, a2, bgate, cw_ref, cb_ref, wdown_ref)
    if final:
        y = _rmsnorm(y, fng_ref[...])
    xo_ref[...] = y.reshape(nb, ts, d)


def _channel_sample(x, mod, state_conv, lw, final_g, final):
    nb, ts, d = x.shape
    kernel = functools.partial(_channel_sample_kernel, final=final)
    args = (x, mod, state_conv, lw["norm2_g"], lw["ffn_w_up"], lw["ffn_conv_w"], lw["ffn_conv_b"],
            lw["ffn_w_down"], final_g)
    out_shape = (
        jax.ShapeDtypeStruct((nb, ts, d), F32),
        jax.ShapeDtypeStruct((nb, CONV_W - 1, D_FF), F32),
    )
    return pl.pallas_call(
        kernel,
        grid=(1,),
        in_specs=[_full(a.shape) for a in args],
        out_specs=tuple(_full(s.shape) for s in out_shape),
        out_shape=out_shape,
        scratch_shapes=[pltpu.VMEM((nb, CONV_PAD + ts, D_FF), F32)],
        compiler_params=pltpu.CompilerParams(
            dimension_semantics=("arbitrary",),
            vmem_limit_bytes=VMEM_LIMIT_BYTES),
        name="channel_sample",
    )(*args)


def _time_tile(seq):
    for tt in (256, 128):
        if seq % tt == 0:
            return tt
    raise ValueError(f"prompt length {seq} must be a multiple of {SGU_LEN}")


def kernel(x_prompt, x_sample, c_prompt, c_sample, cache_k_win, cache_v_win, state_pool, state_ffn_conv,
           norm1_g, norm2_g, w_ada, b_ada, w_in, attn_sink, w_o_attn, pool_w, pool_scale, w_o_pool,
           sgu_norm_g, sgu_w, sgu_b, w_o_sgu, w_out, ffn_w_up, ffn_conv_w, ffn_conv_b, ffn_w_down,
           final_norm_g):
    depth = w_in.shape[0]
    bp, seq, d = x_prompt.shape
    bs, ts, _ = x_sample.shape
    wc = cache_k_win.shape[2]
    assert d == D_MODEL and bp + bs <= MOD_ROWS
    assert seq >= WINDOW and ts % SUBLANES == 0 and ts >= POOL_STATE and wc + ts <= 2 * LANES
    tt = _time_tile(seq)

    c_all = jnp.concatenate(
        [c_prompt, c_sample, jnp.zeros((MOD_ROWS - bp - bs, d), F32)], axis=0)
    mod = _modulation(c_all, w_ada, b_ada)
    mod_p = mod[:, 0:bp].reshape(depth, bp, 1, 6 * d)
    mod_s = mod[:, bp:bp + bs].reshape(depth, bs, 1, 6 * d)

    bf = lambda w: w.astype(BF16)
    params = dict(
        norm1_g=norm1_g.reshape(depth, 1, d), norm2_g=norm2_g.reshape(depth, 1, d),
        w_in=bf(w_in), attn_sink=attn_sink, w_o_attn=bf(w_o_attn), pool_w=bf(pool_w),
        pool_scale=pool_scale.reshape(depth, 1, POOL_WIDTH), w_o_pool=bf(w_o_pool),
        sgu_norm_g=sgu_norm_g.reshape(depth, 1, SGU_WIDTH), sgu_w=bf(sgu_w),
        sgu_b_t=jnp.transpose(sgu_b, (0, 2, 1)), w_o_sgu=bf(w_o_sgu), w_out=bf(w_out),
        ffn_w_up=bf(ffn_w_up), ffn_conv_w=ffn_conv_w,
        ffn_conv_b=ffn_conv_b.reshape(depth, 1, D_FF), ffn_w_down=bf(ffn_w_down))
    final_g = final_norm_g.reshape(1, d)

    ck = cache_k_win.reshape(depth, bs, wc, KV_WIDTH)
    cv = cache_v_win.reshape(depth, bs, wc, KV_WIDTH)

    xp, xs = x_prompt, x_sample
    kp, vp, pp, cp = [], [], [], []
    ks_, vs_, ps_, cs_, ss_ = [], [], [], [], []
    for l in range(depth):
        lw = {name: p[l] for name, p in params.items()}
        final = l == depth - 1
        xp, nk, nv, npool = _tokmix_prompt(xp, mod_p[l], lw, tt)
        xp, nconv = _channel_prompt(xp, mod_p[l], lw, final_g, tt, final)
        kp.append(nk); vp.append(nv); pp.append(npool); cp.append(nconv)
        xs, nk, nv, npool, nsv = _tokmix_sample(xs, mod_s[l], ck[l], cv[l], state_pool[l], lw, PAST_LEN)
        xs, nconv = _channel_sample(xs, mod_s[l], state_ffn_conv[l], lw, final_g, final)
        ks_.append(nk); vs_.append(nv); ps_.append(npool); cs_.append(nconv); ss_.append(nsv)

    kv_shape_p = (depth, bp, WINDOW, N_KV_HEADS, HEAD_DIM)
    kv_shape_s = (depth, bs, wc, N_KV_HEADS, HEAD_DIM)
    return (xp, xs,
            jnp.stack(kp).reshape(kv_shape_p), jnp.stack(vp).reshape(kv_shape_p),
            jnp.stack(pp), jnp.stack(cp),
            jnp.stack(ks_).reshape(kv_shape_s), jnp.stack(vs_).reshape(kv_shape_s),
            jnp.stack(ps_), jnp.stack(cs_), jnp.stack(ss_))
```

```python
import functools

import jax
import jax.numpy as jnp
from jax import lax
from jax.experimental import pallas as pl
from jax.experimental.pallas import tpu as pltpu

F32 = jnp.float32
BF16 = jnp.bfloat16

D_MODEL = 1024
CHUNK = 64
EPS = 1e-6
N_HEADS = 16
N_KV_HEADS = 2
GROUP = N_HEADS // N_KV_HEADS
HEAD_DIM = 64
WINDOW = 128
WINDOW_CHUNKS = WINDOW // CHUNK
ATTN_WIDTH = N_HEADS * HEAD_DIM
KV_WIDTH = N_KV_HEADS * HEAD_DIM
POOL_WINDOWS = (2, 4, 8, 16)
POOL_GROUPS = 4
POOL_GROUP_WIDTH = D_MODEL // 8
POOL_WIDTH = POOL_GROUPS * POOL_GROUP_WIDTH
POOL_STATE = 16 - 1
SGU_LEN = 128
SGU_GROUPS = 4
SGU_GROUP_WIDTH = D_MODEL // 8
SGU_WIDTH = SGU_GROUPS * SGU_GROUP_WIDTH
N_BRANCH = 3
Q_END = ATTN_WIDTH
K_END = Q_END + KV_WIDTH
V_END = K_END + KV_WIDTH
POOL_END = V_END + POOL_WIDTH
SU_END = POOL_END + SGU_WIDTH
SV_END = SU_END + SGU_WIDTH
N_IN = SV_END + N_BRANCH * D_MODEL
D_FF = 2816
CONV_W = 3
PAST_LEN = 1024

LANES = 128
SUBLANES = 8
HALF_LANES = LANES // 2
VMEM_LIMIT_BYTES = 56 * 1024 * 1024

NEG_BIG = -1e30
PAIRS = GROUP // 2
KEY_WIN = 4 * CHUNK
PREV_ROWS = KEY_WIN - CHUNK
POOL_PAD = 16
CONV_PAD = SUBLANES
MOD_ROWS = 16
PROJ_SLAB = 2 * 256


def _alibi_slope(head):
    return 2.0 ** (-8.0 * (head + 1) / N_HEADS)


def _rmsnorm(x, g):
    return x * lax.rsqrt(jnp.mean(x * x, axis=-1, keepdims=True) + EPS) * g


def _dot(a, b):
    return jnp.dot(a, b, preferred_element_type=F32)


def _dot_nt(a, b):
    return lax.dot_general(a, b, (((1,), (1,)), ((), ())), preferred_element_type=F32)


def _mod_kernel(c_ref, w_ref, b_ref, o_ref):
    c = c_ref[...]
    sc = c * jax.nn.sigmoid(c)
    o_ref[0] = jnp.dot(sc, w_ref[0], preferred_element_type=F32,
                       precision=lax.Precision.HIGHEST) + b_ref[0]


def _modulation(c_all, w_ada, b_ada):
    depth, d, n = w_ada.shape
    bn = n // 4
    return pl.pallas_call(
        _mod_kernel,
        grid=(depth, n // bn),
        in_specs=[
            pl.BlockSpec((MOD_ROWS, d), lambda l, j: (0, 0)),
            pl.BlockSpec((1, d, bn), lambda l, j: (l, 0, j)),
            pl.BlockSpec((1, 1, bn), lambda l, j: (l, 0, j)),
        ],
        out_specs=pl.BlockSpec((1, MOD_ROWS, bn), lambda l, j: (l, 0, j)),
        out_shape=jax.ShapeDtypeStruct((depth, MOD_ROWS, n), F32),
        compiler_params=pltpu.CompilerParams(
            dimension_semantics=("arbitrary", "arbitrary"),
            vmem_limit_bytes=VMEM_LIMIT_BYTES),
        name="adaln_modulation",
    )(c_all, w_ada, b_ada.reshape(depth, 1, n))


def _split_heads(z):
    lane = lax.broadcasted_iota(jnp.int32, z.shape, 1)
    low = lane < HALF_LANES
    zs = pltpu.roll(z, HALF_LANES, axis=1)
    zero = jnp.zeros_like(z)
    top0 = jnp.where(low, z, zero)
    bot0 = jnp.where(low, zero, zs)
    top1 = jnp.where(low, zs, zero)
    bot1 = jnp.where(low, zero, z)
    return top0, bot0, top1, bot1


def _zero_after(*results):
    bits = None
    for r in results:
        b = lax.bitcast_convert_type(r[0:SUBLANES, 0:LANES], jnp.uint32)
        bits = b if bits is None else bits | b
    bits = lax.shift_right_logical(lax.shift_right_logical(bits, jnp.uint32(16)), jnp.uint32(16))
    return lax.bitcast_convert_type(bits, F32)[0:1, 0:1]


def _attn_logits(qs, kt):
    return _dot_nt(qs, kt)


def _attn_finish(logits, vb, bias, sink_a, sink_b):
    w = vb.shape[0] // 2
    logits = logits + bias
    la = logits[:, :w]
    lb = logits[:, w:]
    ma = jnp.maximum(jnp.max(la, axis=-1, keepdims=True), sink_a)
    mb = jnp.maximum(jnp.max(lb, axis=-1, keepdims=True), sink_b)
    pa = jnp.exp(la - ma)
    pb = jnp.exp(lb - mb)
    da = jnp.sum(pa, axis=-1, keepdims=True) + jnp.exp(sink_a - ma)
    db = jnp.sum(pb, axis=-1, keepdims=True) + jnp.exp(sink_b - mb)
    p = jnp.concatenate([pa, pb], axis=1).astype(BF16)
    o = _dot(p, vb)
    lane = lax.broadcasted_iota(jnp.int32, o.shape, 1)
    inv = jnp.where(lane < HALF_LANES, 1.0 / da, 1.0 / db)
    return o * inv


def _pool_window_sums(e):
    outs = []
    for gi, w in enumerate(POOL_WINDOWS):
        s = e[:, gi * POOL_GROUP_WIDTH:(gi + 1) * POOL_GROUP_WIDTH]
        k = 1
        while k < w:
            s = s + pltpu.roll(s, k, axis=0)
            k *= 2
        outs.append(s)
    return outs


def _pool_branch(sums, pu, pos, poolw_ref, pscale):
    outs = []
    for gi, w in enumerate(POOL_WINDOWS):
        c0, c1 = gi * POOL_GROUP_WIDTH, (gi + 1) * POOL_GROUP_WIDTH
        cnt = jnp.minimum(pos + 1, w).astype(F32)
        d = sums[gi] / cnt - pu[:, c0:c1]
        outs.append(_dot(d.astype(BF16), poolw_ref[gi]))
    return jnp.concatenate(outs, axis=1) * pscale


def _sgu_weight(sguw_ref, g, length):
    i = lax.broadcasted_iota(jnp.int32, (length, length), 0)
    j = lax.broadcasted_iota(jnp.int32, (length, length), 1)
    w = sguw_ref[g, :length, :length]
    return jnp.where((j // CHUNK) <= (i // CHUNK), w, jnp.zeros_like(w))


def _merge_and_project(x, g1, zg, ya, yb, yc, woa_ref, wop_ref, wos_ref, wout_ref):
    ga = zg[:, :D_MODEL]
    gb = zg[:, D_MODEL:2 * D_MODEL]
    gc = zg[:, 2 * D_MODEL:]
    merged = (jax.nn.sigmoid(ga) * _dot(ya, woa_ref[...])
              + jax.nn.sigmoid(gb) * _dot(yb.astype(BF16), wop_ref[...])
              + jax.nn.sigmoid(gc) * _dot(yc.astype(BF16), wos_ref[...]))
    return x + g1 * _dot(merged.astype(BF16), wout_ref[...])


def _build_prompt_bias(bias_tab):
    q = lax.broadcasted_iota(jnp.int32, (CHUNK, KEY_WIN), 0)
    j = lax.broadcasted_iota(jnp.int32, (CHUNK, KEY_WIN), 1)
    dist = jnp.abs(PREV_ROWS + q - j).astype(F32)
    for variant, first_valid in enumerate((3 * CHUNK, 2 * CHUNK, CHUNK)):
        valid = j >= first_valid
        for kv in range(N_KV_HEADS):
            for p in range(PAIRS):
                for par in range(2):
                    slope = _alibi_slope(kv * GROUP + 2 * p + par)
                    blk = jnp.where(valid, -slope * dist, NEG_BIG)
                    bias_tab[variant, kv, p * CHUNK:(p + 1) * CHUNK,
                             par * KEY_WIN:(par + 1) * KEY_WIN] = blk


def _tokmix_prompt_kernel(x_ref, mod_ref, n1g_ref, win_ref, woa_ref, wop_ref, wos_ref, wout_ref,
                          poolw_ref, pscale_ref, sgug_ref, sguw_ref, sgub_ref, sink_ref,
                          xo_ref, nk_ref, nv_ref, npool_ref,
                          kext, vext, pext, ya_scr, bias_tab, *, tt):
    b = pl.program_id(0)
    t = pl.program_id(1)
    n_chunks = tt // CHUNK

    @pl.when(jnp.logical_and(b == 0, t == 0))
    def _():
        _build_prompt_bias(bias_tab)

    @pl.when(t == 0)
    def _():
        kext[:, 0:PREV_ROWS, :] = jnp.zeros((4, PREV_ROWS, LANES), BF16)
        vext[:, 0:PREV_ROWS, :] = jnp.zeros((4, PREV_ROWS, LANES), BF16)
        pext[0:POOL_PAD, :] = jnp.zeros((POOL_PAD, POOL_WIDTH), F32)

    x = x_ref[0]
    sh1 = mod_ref[0, :, 0:D_MODEL]
    sc1 = mod_ref[0, :, D_MODEL:2 * D_MODEL]
    g1 = mod_ref[0, :, 2 * D_MODEL:3 * D_MODEL]
    h = _rmsnorm(x, n1g_ref[...]) * (1.0 + sc1) + sh1
    hb = h.astype(BF16)

    zqkv = _dot(hb, win_ref[:, 0:V_END])
    zk = zqkv[:, Q_END:K_END]
    zv = zqkv[:, K_END:V_END]
    nk_ref[0] = zk[tt - WINDOW:, :]
    nv_ref[0] = zv[tt - WINDOW:, :]
    qb = (zqkv[:, 0:Q_END] * (HEAD_DIM ** -0.5)).astype(BF16)
    for idx, part in enumerate(_split_heads(zk)):
        kext[idx, PREV_ROWS:PREV_ROWS + tt, :] = part.astype(BF16)
    for idx, part in enumerate(_split_heads(zv)):
        vext[idx, PREV_ROWS:PREV_ROWS + tt, :] = part.astype(BF16)

    slab_starts = list(range(V_END, N_IN, PROJ_SLAB))
    slabs = []

    def project_slab():
        c0 = slab_starts[len(slabs)]
        slabs.append(_dot(hb, win_ref[:, c0:c0 + PROJ_SLAB]))
        return slabs[-1]

    def pool_projection():
        pu = slabs[0]
        npool_ref[0] = pu[tt - POOL_STATE:, :]
        pext[POOL_PAD:POOL_PAD + tt, :] = pu
        sums = [s[POOL_PAD:, :] for s in _pool_window_sums(pext[...])]
        pos = t * tt + lax.broadcasted_iota(jnp.int32, (tt, POOL_GROUP_WIDTH), 0)
        yb = _pool_branch(sums, pu, pos, poolw_ref, pscale_ref[...])
        pext[0:POOL_PAD, :] = pext[tt:tt + POOL_PAD, :]
        return _dot(yb.astype(BF16), wop_ref[...])

    def block_logits(i, kv, zero=None):
        r0 = i * CHUNK
        qs = jnp.concatenate(
            [qb[r0:r0 + CHUNK, (kv * PAIRS + p) * LANES:(kv * PAIRS + p + 1) * LANES]
             for p in range(PAIRS)], axis=0)
        kt = jnp.concatenate([kext[2 * kv, r0:r0 + KEY_WIN, :],
                              kext[2 * kv + 1, r0:r0 + KEY_WIN, :]], axis=0)
        if zero is not None:
            kt = kt + zero.astype(BF16)
        return _attn_logits(qs, kt)

    def block_finish(i, kv, logits, zero):
        r0 = i * CHUNK
        variant = jnp.where(t == 0, min(i, 2), 2) if i < 2 else 2
        vb = jnp.concatenate([vext[2 * kv, r0:r0 + KEY_WIN, :],
                              vext[2 * kv + 1, r0:r0 + KEY_WIN, :]], axis=0)
        vb = vb + zero.astype(BF16)
        o = _attn_finish(logits, vb, bias_tab[variant, kv], sink_ref[kv, 0], sink_ref[kv, 1])
        for p in range(PAIRS):
            ya_scr[r0:r0 + CHUNK, (kv * PAIRS + p) * LANES:(kv * PAIRS + p + 1) * LANES] = (
                o[p * CHUNK:(p + 1) * CHUNK, :].astype(BF16))

    blocks = [(i, kv) for i in range(n_chunks) for kv in range(N_KV_HEADS)]
    n_lead = len(slab_starts) + 1 - len(blocks)
    assert n_lead >= 1
    for _ in range(n_lead):
        lead = project_slab()
    logits = block_logits(*blocks[0], _zero_after(lead))
    merged_b = None
    for n, blk in enumerate(blocks):
        ahead = []
        nxt = None
        if n + 1 < len(blocks):
            nxt = block_logits(*blocks[n + 1])
            ahead.append(nxt)
        if len(slabs) < len(slab_starts):
            ahead.append(project_slab())
        else:
            merged_b = pool_projection()
            ahead.append(merged_b)
        block_finish(*blk, logits, _zero_after(*ahead))
        logits = nxt

    kext[:, 0:PREV_ROWS, :] = kext[:, tt:tt + PREV_ROWS, :]
    vext[:, 0:PREV_ROWS, :] = vext[:, tt:tt + PREV_ROWS, :]

    zsgu = jnp.concatenate(slabs[1:3], axis=1)
    uu = jax.nn.gelu(zsgu[:, 0:SGU_WIDTH])
    vn = _rmsnorm(jax.nn.gelu(zsgu[:, SGU_WIDTH:]), sgug_ref[...])
    vnb = vn.astype(BF16)
    n_sgu = tt // SGU_LEN
    yc_cols = []
    for g in range(SGU_GROUPS):
        c0, c1 = g * SGU_GROUP_WIDTH, (g + 1) * SGU_GROUP_WIDTH
        rhs = jnp.concatenate([vnb[n * SGU_LEN:(n + 1) * SGU_LEN, c0:c1] for n in range(n_sgu)], axis=1)
        s = _dot(_sgu_weight(sguw_ref, g, SGU_LEN), rhs) + sgub_ref[:, g:g + 1]
        s_rows = jnp.concatenate([s[:, n * SGU_GROUP_WIDTH:(n + 1) * SGU_GROUP_WIDTH]
                                  for n in range(n_sgu)], axis=0)
        yc_cols.append(uu[:, c0:c1] * s_rows)
    yc = jnp.concatenate(yc_cols, axis=1)

    zg = jnp.concatenate(slabs[3:], axis=1)
    merged = (jax.nn.sigmoid(zg[:, :D_MODEL]) * _dot(ya_scr[...], woa_ref[...])
              + jax.nn.sigmoid(zg[:, D_MODEL:2 * D_MODEL]) * merged_b
              + jax.nn.sigmoid(zg[:, 2 * D_MODEL:]) * _dot(yc.astype(BF16), wos_ref[...]))
    xo_ref[0] = x + g1 * _dot(merged.astype(BF16), wout_ref[...])


def _resident(shape, layer=None):
    if layer is None:
        return pl.BlockSpec(shape, lambda *_: (0,) * len(shape), pipeline_mode=pl.Buffered(1))
    return pl.BlockSpec((None,) + tuple(shape), lambda *_: (layer,) + (0,) * len(shape),
                        pipeline_mode=pl.Buffered(1))


def _tokmix_prompt(x, mod, lw, layer, tt):
    bsz, seq, d = x.shape
    nt = seq // tt
    rows = PAIRS * CHUNK
    kernel = functools.partial(_tokmix_prompt_kernel, tt=tt)
    out_shape = (
        jax.ShapeDtypeStruct((bsz, seq, d), F32),
        jax.ShapeDtypeStruct((bsz, WINDOW, KV_WIDTH), F32),
        jax.ShapeDtypeStruct((bsz, WINDOW, KV_WIDTH), F32),
        jax.ShapeDtypeStruct((bsz, POOL_STATE, POOL_WIDTH), F32),
    )
    return pl.pallas_call(
        kernel,
        grid=(bsz, nt),
        in_specs=[
            pl.BlockSpec((1, tt, d), lambda b, t: (b, t, 0)),
            pl.BlockSpec((1, 1, 6 * d), lambda b, t: (b, 0, 0)),
            _resident((1, d), layer),
            _resident((d, N_IN), layer),
            _resident((ATTN_WIDTH, d), layer),
            _resident((POOL_WIDTH, d), layer),
            _resident((SGU_WIDTH, d), layer),
            _resident((d, d), layer),
            _resident((POOL_GROUPS, POOL_GROUP_WIDTH, POOL_GROUP_WIDTH), layer),
            _resident((1, POOL_WIDTH), layer),
            _resident((1, SGU_WIDTH), layer),
            _resident((SGU_GROUPS, SGU_LEN, SGU_LEN), layer),
            _resident((SGU_LEN, SGU_GROUPS), layer),
            _resident((N_KV_HEADS, 2, rows, 1)),
        ],
        out_specs=(
            pl.BlockSpec((1, tt, d), lambda b, t: (b, t, 0)),
            pl.BlockSpec((1, WINDOW, KV_WIDTH), lambda b, t: (b, 0, 0)),
            pl.BlockSpec((1, WINDOW, KV_WIDTH), lambda b, t: (b, 0, 0)),
            pl.BlockSpec((1, POOL_STATE, POOL_WIDTH), lambda b, t: (b, 0, 0)),
        ),
        out_shape=out_shape,
        scratch_shapes=[
            pltpu.VMEM((4, PREV_ROWS + tt, LANES), BF16),
            pltpu.VMEM((4, PREV_ROWS + tt, LANES), BF16),
            pltpu.VMEM((POOL_PAD + tt, POOL_WIDTH), F32),
            pltpu.VMEM((tt, ATTN_WIDTH), BF16),
            pltpu.VMEM((3, N_KV_HEADS, rows, 2 * KEY_WIN), F32),
        ],
        compiler_params=pltpu.CompilerParams(
            dimension_semantics=("arbitrary", "arbitrary"),
            vmem_limit_bytes=VMEM_LIMIT_BYTES),
        name="tokmix_prompt",
    )(x, mod, lw["norm1_g"], lw["w_in"], lw["w_o_attn"], lw["w_o_pool"], lw["w_o_sgu"], lw["w_out"],
      lw["pool_w"], lw["pool_scale"], lw["sgu_norm_g"], lw["sgu_w"], lw["sgu_b_t"],
      _sink_table(lw["attn_sink"][layer], CHUNK))


def _sink_table(sink, rows_per_pair):
    s = sink.astype(F32).reshape(N_KV_HEADS, PAIRS, 2)
    s = jnp.transpose(s, (0, 2, 1))
    s = jnp.repeat(s, rows_per_pair, axis=2)
    return s[..., None]


def _conv_ffn_tail(x, g2, a, a1, a2, bgate, cw_ref, cb_ref, wdown_ref):
    conv = a2 * cw_ref[0:1, :] + a1 * cw_ref[1:2, :] + a * cw_ref[2:3, :] + cb_ref[...]
    f = _dot((jax.nn.gelu(conv) * bgate).astype(BF16), wdown_ref[...])
    return x + g2 * f


def _channel_prompt_kernel(x_ref, mod_ref, n2g_ref, wup_ref, cw_ref, cb_ref, wdown_ref, fng_ref,
                           xo_ref, nconv_ref, aext, *, tt, final):
    t = pl.program_id(1)

    @pl.when(t == 0)
    def _():
        aext[0:CONV_PAD, :] = jnp.zeros((CONV_PAD, D_FF), F32)

    x = x_ref[0]
    sh2 = mod_ref[0, :, 3 * D_MODEL:4 * D_MODEL]
    sc2 = mod_ref[0, :, 4 * D_MODEL:5 * D_MODEL]
    g2 = mod_ref[0, :, 5 * D_MODEL:6 * D_MODEL]
    h = _rmsnorm(x, n2g_ref[...]) * (1.0 + sc2) + sh2
    up = _dot(h.astype(BF16), wup_ref[...])
    a = up[:, 0:D_FF]
    bgate = up[:, D_FF:]
    nconv_ref[0] = a[tt - (CONV_W - 1):, :]
    aext[CONV_PAD:CONV_PAD + tt, :] = a
    a1 = aext[CONV_PAD - 1:CONV_PAD - 1 + tt, :]
    a2 = aext[CONV_PAD - 2:CONV_PAD - 2 + tt, :]
    y = _conv_ffn_tail(x, g2, a, a1, a2, bgate, cw_ref, cb_ref, wdown_ref)
    aext[0:CONV_PAD, :] = aext[tt:tt + CONV_PAD, :]
    if final:
        y = _rmsnorm(y, fng_ref[...])
    xo_ref[0] = y


def _channel_prompt(x, mod, lw, layer, final_g, tt, final):
    bsz, seq, d = x.shape
    nt = seq // tt
    kernel = functools.partial(_channel_prompt_kernel, tt=tt, final=final)
    return pl.pallas_call(
        kernel,
        grid=(bsz, nt),
        in_specs=[
            pl.BlockSpec((1, tt, d), lambda b, t: (b, t, 0)),
            pl.BlockSpec((1, 1, 6 * d), lambda b, t: (b, 0, 0)),
            _resident((1, d), layer),
            _resident((d, 2 * D_FF), layer),
            _resident((CONV_W, D_FF), layer),
            _resident((1, D_FF), layer),
            _resident((D_FF, d), layer),
            _resident((1, d)),
        ],
        out_specs=(
            pl.BlockSpec((1, tt, d), lambda b, t: (b, t, 0)),
            pl.BlockSpec((1, CONV_W - 1, D_FF), lambda b, t: (b, 0, 0)),
        ),
        out_shape=(
            jax.ShapeDtypeStruct((bsz, seq, d), F32),
            jax.ShapeDtypeStruct((bsz, CONV_W - 1, D_FF), F32),
        ),
        scratch_shapes=[pltpu.VMEM((CONV_PAD + tt, D_FF), F32)],
        compiler_params=pltpu.CompilerParams(
            dimension_semantics=("arbitrary", "arbitrary"),
            vmem_limit_bytes=VMEM_LIMIT_BYTES),
        name="channel_prompt",
    )(x, mod, lw["norm2_g"], lw["ffn_w_up"], lw["ffn_conv_w"], lw["ffn_conv_b"], lw["ffn_w_down"],
      final_g)


def _tokmix_sample_kernel(x_ref, mod_ref, ck_ref, cv_ref, sp_ref, n1g_ref, win_ref, woa_ref, wop_ref,
                          wos_ref, wout_ref, poolw_ref, pscale_ref, sgug_ref, sguw_ref, sgub_ref,
                          sink_ref,
                          xo_ref, nk_ref, nv_ref, npool_ref, vn_ref,
                          ya_scr, *, pos0):
    nb, ts, d = x_ref.shape
    wc = ck_ref.shape[1]
    n_keys = wc + ts
    key_win = 2 * LANES

    x3 = x_ref[...]
    sh1 = mod_ref[:, :, 0:D_MODEL]
    sc1 = mod_ref[:, :, D_MODEL:2 * D_MODEL]
    g1 = mod_ref[:, :, 2 * D_MODEL:3 * D_MODEL]
    h3 = _rmsnorm(x3, n1g_ref[...]) * (1.0 + sc1) + sh1
    hb = h3.reshape(nb * ts, d).astype(BF16)

    zqkv = _dot(hb, win_ref[:, 0:V_END])
    qb = (zqkv[:, 0:Q_END] * (HEAD_DIM ** -0.5)).astype(BF16)

    q = lax.broadcasted_iota(jnp.int32, (ts, key_win), 0)
    j = lax.broadcasted_iota(jnp.int32, (ts, key_win), 1)
    dist = jnp.abs(wc + q - j).astype(F32)
    valid = jnp.logical_and(j < n_keys, (pos0 - wc + j) >= 0)
    bias = []
    for kv in range(N_KV_HEADS):
        blocks = []
        for par in range(2):
            col = [jnp.where(valid, -_alibi_slope(kv * GROUP + 2 * p + par) * dist, NEG_BIG)
                   for p in range(PAIRS)]
            blocks.append(jnp.concatenate(col, axis=0))
        bias.append(jnp.concatenate(blocks, axis=1))

    pad = jnp.zeros((key_win - n_keys, KV_WIDTH), F32)
    for b in range(nb):
        r0 = b * ts
        zk = zqkv[r0:r0 + ts, Q_END:K_END]
        zv = zqkv[r0:r0 + ts, K_END:V_END]
        ck = ck_ref[b]
        cv = cv_ref[b]
        nk_ref[b] = jnp.concatenate([ck[ts:, :], zk], axis=0)
        nv_ref[b] = jnp.concatenate([cv[ts:, :], zv], axis=0)
        ksplit = [part.astype(BF16) for part in _split_heads(jnp.concatenate([ck, zk, pad], axis=0))]
        vsplit = [part.astype(BF16) for part in _split_heads(jnp.concatenate([cv, zv, pad], axis=0))]
        for kv in range(N_KV_HEADS):
            qs = jnp.concatenate(
                [qb[r0:r0 + ts, (kv * PAIRS + p) * LANES:(kv * PAIRS + p + 1) * LANES]
                 for p in range(PAIRS)], axis=0)
            kt = jnp.concatenate([ksplit[2 * kv], ksplit[2 * kv + 1]], axis=0)
            vb = jnp.concatenate([vsplit[2 * kv], vsplit[2 * kv + 1]], axis=0)
            o = _attn_finish(_attn_logits(qs, kt), vb, bias[kv], sink_ref[kv, 0], sink_ref[kv, 1])
            for p in range(PAIRS):
                ya_scr[r0:r0 + ts, (kv * PAIRS + p) * LANES:(kv * PAIRS + p + 1) * LANES] = (
                    o[p * ts:(p + 1) * ts, :].astype(BF16))

    zmix = _dot(hb, win_ref[:, V_END:SV_END])
    pu = zmix[:, 0:POOL_WIDTH]
    head = jnp.zeros((POOL_PAD - POOL_STATE, POOL_WIDTH), F32)
    sums = [[] for _ in POOL_WINDOWS]
    for b in range(nb):
        pub = pu[b * ts:(b + 1) * ts, :]
        npool_ref[b] = pub[ts - POOL_STATE:, :]
        e = jnp.concatenate([head, sp_ref[b], pub], axis=0)
        for gi, s in enumerate(_pool_window_sums(e)):
            sums[gi].append(s[POOL_PAD:, :])
    sums = [jnp.concatenate(s, axis=0) for s in sums]
    pos = pos0 + lax.broadcasted_iota(jnp.int32, (nb, ts, POOL_GROUP_WIDTH), 1).reshape(
        nb * ts, POOL_GROUP_WIDTH)
    yb = _pool_branch(sums, pu, pos, poolw_ref, pscale_ref[...])

    uu = jax.nn.gelu(zmix[:, POOL_WIDTH:POOL_WIDTH + SGU_WIDTH])
    vn = _rmsnorm(jax.nn.gelu(zmix[:, POOL_WIDTH + SGU_WIDTH:]), sgug_ref[...])
    vn_ref[...] = vn.reshape(nb, ts, SGU_WIDTH)
    vnb = vn.astype(BF16)
    yc_cols = []
    for g in range(SGU_GROUPS):
        c0, c1 = g * SGU_GROUP_WIDTH, (g + 1) * SGU_GROUP_WIDTH
        rhs = jnp.concatenate([vnb[b * ts:(b + 1) * ts, c0:c1] for b in range(nb)], axis=1)
        s = _dot(_sgu_weight(sguw_ref, g, ts), rhs) + sgub_ref[0:ts, g:g + 1]
        s_rows = jnp.concatenate([s[:, b * SGU_GROUP_WIDTH:(b + 1) * SGU_GROUP_WIDTH]
                                  for b in range(nb)], axis=0)
        yc_cols.append(uu[:, c0:c1] * s_rows)
    yc = jnp.concatenate(yc_cols, axis=1)

    x2 = x3.reshape(nb * ts, d)
    g1r = jnp.broadcast_to(g1, (nb, ts, d)).reshape(nb * ts, d)
    zg = _dot(hb, win_ref[:, SV_END:N_IN])
    y = _merge_and_project(x2, g1r, zg, ya_scr[...], yb, yc, woa_ref, wop_ref, wos_ref, wout_ref)
    xo_ref[...] = y.reshape(nb, ts, d)


def _full(shape, layer=None):
    if layer is None:
        return pl.BlockSpec(shape, lambda *_: (0,) * len(shape))
    return pl.BlockSpec((None,) + tuple(shape[1:]), lambda *_: (layer,) + (0,) * (len(shape) - 1))


def _tokmix_sample(x, mod, cache_k, cache_v, state_pool, lw, layer, pos0):
    nb, ts, d = x.shape
    wc = cache_k.shape[1]
    kernel = functools.partial(_tokmix_sample_kernel, pos0=pos0)
    per_layer = (lw["norm1_g"], lw["w_in"], lw["w_o_attn"], lw["w_o_pool"], lw["w_o_sgu"], lw["w_out"],
                 lw["pool_w"], lw["pool_scale"], lw["sgu_norm_g"], lw["sgu_w"], lw["sgu_b_t"])
    shared = (x, mod, cache_k, cache_v, state_pool)
    sink = _sink_table(lw["attn_sink"][layer], ts)
    args = shared + per_layer + (sink,)
    in_specs = ([_full(a.shape) for a in shared] + [_full(a.shape, layer) for a in per_layer]
                + [_full(sink.shape)])
    out_shape = (
        jax.ShapeDtypeStruct((nb, ts, d), F32),
        jax.ShapeDtypeStruct((nb, wc, KV_WIDTH), F32),
        jax.ShapeDtypeStruct((nb, wc, KV_WIDTH), F32),
        jax.ShapeDtypeStruct((nb, POOL_STATE, POOL_WIDTH), F32),
        jax.ShapeDtypeStruct((nb, ts, SGU_WIDTH), F32),
    )
    return pl.pallas_call(
        kernel,
        grid=(1,),
        in_specs=in_specs,
        out_specs=tuple(_full(s.shape) for s in out_shape),
        out_shape=out_shape,
        scratch_shapes=[pltpu.VMEM((nb * ts, ATTN_WIDTH), BF16)],
        compiler_params=pltpu.CompilerParams(
            dimension_semantics=("arbitrary",),
            vmem_limit_bytes=VMEM_LIMIT_BYTES),
        name="tokmix_sample",
    )(*args)


def _channel_sample_kernel(x_ref, mod_ref, sc_ref, n2g_ref, wup_ref, cw_ref, cb_ref, wdown_ref, fng_ref,
                           xo_ref, nconv_ref, aext, *, final):
    nb, ts, d = x_ref.shape
    x3 = x_ref[...]
    sh2 = mod_ref[:, :, 3 * D_MODEL:4 * D_MODEL]
    sc2 = mod_ref[:, :, 4 * D_MODEL:5 * D_MODEL]
    g2 = mod_ref[:, :, 5 * D_MODEL:6 * D_MODEL]
    h3 = _rmsnorm(x3, n2g_ref[...]) * (1.0 + sc2) + sh2
    up = _dot(h3.reshape(nb * ts, d).astype(BF16), wup_ref[...])
    a = up[:, 0:D_FF]
    bgate = up[:, D_FF:]
    a1s, a2s = [], []
    for b in range(nb):
        ab = a[b * ts:(b + 1) * ts, :]
        nconv_ref[b] = ab[ts - (CONV_W - 1):, :]
        aext[b, CONV_PAD - (CONV_W - 1):CONV_PAD, :] = sc_ref[b]
        aext[b, CONV_PAD:CONV_PAD + ts, :] = ab
        a1s.append(aext[b, CONV_PAD - 1:CONV_PAD - 1 + ts, :])
        a2s.append(aext[b, CONV_PAD - 2:CONV_PAD - 2 + ts, :])
    a1 = jnp.concatenate(a1s, axis=0)
    a2 = jnp.concatenate(a2s, axis=0)
    x2 = x3.reshape(nb * ts, d)
    g2r = jnp.broadcast_to(g2, (nb, ts, d)).reshape(nb * ts, d)
    y = _conv_ffn_tail(x2, g2r, a, a1, a2, bgate, cw_ref, cb_ref, wdown_ref)
    if final:
        y = _rmsnorm(y, fng_ref[...])
    xo_ref[...] = y.reshape(nb, ts, d)


def _channel_sample(x, mod, state_conv, lw, layer, final_g, final):
    nb, ts, d = x.shape
    kernel = functools.partial(_channel_sample_kernel, final=final)
    per_layer = (lw["norm2_g"], lw["ffn_w_up"], lw["ffn_conv_w"], lw["ffn_conv_b"], lw["ffn_w_down"])
    shared = (x, mod, state_conv)
    args = shared + per_layer + (final_g,)
    in_specs = ([_full(a.shape) for a in shared] + [_full(a.shape, layer) for a in per_layer]
                + [_full(final_g.shape)])
    out_shape = (
        jax.ShapeDtypeStruct((nb, ts, d), F32),
        jax.ShapeDtypeStruct((nb, CONV_W - 1, D_FF), F32),
    )
    return pl.pallas_call(
        kernel,
        grid=(1,),
        in_specs=in_specs,
        out_specs=tuple(_full(s.shape) for s in out_shape),
        out_shape=out_shape,
        scratch_shapes=[pltpu.VMEM((nb, CONV_PAD + ts, D_FF), F32)],
        compiler_params=pltpu.CompilerParams(
            dimension_semantics=("arbitrary",),
            vmem_limit_bytes=VMEM_LIMIT_BYTES),
        name="channel_sample",
    )(*args)


MIX_TILES = (256, 128)
FFN_TILES = (512, 256, 128)


def _time_tile(seq, candidates):
    for tt in candidates:
        if seq % tt == 0:
            return tt
    raise ValueError(f"prompt length {seq} must be a multiple of {candidates[-1]}")


def kernel(x_prompt, x_sample, c_prompt, c_sample, cache_k_win, cache_v_win, state_pool, state_ffn_conv,
           norm1_g, norm2_g, w_ada, b_ada, w_in, attn_sink, w_o_attn, pool_w, pool_scale, w_o_pool,
           sgu_norm_g, sgu_w, sgu_b, w_o_sgu, w_out, ffn_w_up, ffn_conv_w, ffn_conv_b, ffn_w_down,
           final_norm_g):
    depth = w_in.shape[0]
    bp, seq, d = x_prompt.shape
    bs, ts, _ = x_sample.shape
    wc = cache_k_win.shape[2]
    assert d == D_MODEL and bp + bs <= MOD_ROWS
    assert seq >= WINDOW and ts % SUBLANES == 0 and ts >= POOL_STATE and wc + ts <= 2 * LANES
    tt_mix = _time_tile(seq, MIX_TILES)
    tt_ffn = _time_tile(seq, FFN_TILES)

    c_all = jnp.concatenate(
        [c_prompt, c_sample, jnp.zeros((MOD_ROWS - bp - bs, d), F32)], axis=0)
    mod = _modulation(c_all, w_ada, b_ada)
    mod_p = mod[:, 0:bp].reshape(depth, bp, 1, 6 * d)
    mod_s = mod[:, bp:bp + bs].reshape(depth, bs, 1, 6 * d)

    bf = lambda w: w.astype(BF16)
    params = dict(
        norm1_g=norm1_g.reshape(depth, 1, d), norm2_g=norm2_g.reshape(depth, 1, d),
        w_in=bf(w_in), attn_sink=attn_sink, w_o_attn=bf(w_o_attn), pool_w=bf(pool_w),
        pool_scale=pool_scale.reshape(depth, 1, POOL_WIDTH), w_o_pool=bf(w_o_pool),
        sgu_norm_g=sgu_norm_g.reshape(depth, 1, SGU_WIDTH), sgu_w=bf(sgu_w),
        sgu_b_t=jnp.transpose(sgu_b, (0, 2, 1)), w_o_sgu=bf(w_o_sgu), w_out=bf(w_out),
        ffn_w_up=bf(ffn_w_up), ffn_conv_w=ffn_conv_w,
        ffn_conv_b=ffn_conv_b.reshape(depth, 1, D_FF), ffn_w_down=bf(ffn_w_down))
    final_g = final_norm_g.reshape(1, d)

    ck = cache_k_win.reshape(depth, bs, wc, KV_WIDTH)
    cv = cache_v_win.reshape(depth, bs, wc, KV_WIDTH)

    xp, xs = x_prompt, x_sample
    kp, vp, pp, cp = [], [], [], []
    ks_, vs_, ps_, cs_, ss_ = [], [], [], [], []
    for l in range(depth):
        final = l == depth - 1
        xp, nk, nv, npool = _tokmix_prompt(xp, mod_p[l], params, l, tt_mix)
        xp, nconv = _channel_prompt(xp, mod_p[l], params, l, final_g, tt_ffn, final)
        kp.append(nk); vp.append(nv); pp.append(npool); cp.append(nconv)
        xs, nk, nv, npool, nsv = _tokmix_sample(xs, mod_s[l], ck[l], cv[l], state_pool[l], params, l,
                                                PAST_LEN)
        xs, nconv = _channel_sample(xs, mod_s[l], state_ffn_conv[l], params, l, final_g, final)
        ks_.append(nk); vs_.append(nv); ps_.append(npool); cs_.append(nconv); ss_.append(nsv)

    kv_shape_p = (depth, bp, WINDOW, N_KV_HEADS, HEAD_DIM)
    kv_shape_s = (depth, bs, wc, N_KV_HEADS, HEAD_DIM)
    return (xp, xs,
            jnp.stack(kp).reshape(kv_shape_p), jnp.stack(vp).reshape(kv_shape_p),
            jnp.stack(pp), jnp.stack(cp),
            jnp.stack(ks_).reshape(kv_shape_s), jnp.stack(vs_).reshape(kv_shape_s),
            jnp.stack(ps_), jnp.stack(cs_), jnp.stack(ss_))
```

```python
import functools

import jax
import jax.numpy as jnp
from jax import lax
from jax.experimental import pallas as pl
from jax.experimental.pallas import tpu as pltpu

F32 = jnp.float32
BF16 = jnp.bfloat16

D_MODEL = 1024
CHUNK = 64
EPS = 1e-6
N_HEADS = 16
N_KV_HEADS = 2
GROUP = N_HEADS // N_KV_HEADS
HEAD_DIM = 64
WINDOW = 128
WINDOW_CHUNKS = WINDOW // CHUNK
ATTN_WIDTH = N_HEADS * HEAD_DIM
KV_WIDTH = N_KV_HEADS * HEAD_DIM
POOL_WINDOWS = (2, 4, 8, 16)
POOL_GROUPS = 4
POOL_GROUP_WIDTH = D_MODEL // 8
POOL_WIDTH = POOL_GROUPS * POOL_GROUP_WIDTH
POOL_STATE = 16 - 1
SGU_LEN = 128
SGU_GROUPS = 4
SGU_GROUP_WIDTH = D_MODEL // 8
SGU_WIDTH = SGU_GROUPS * SGU_GROUP_WIDTH
N_BRANCH = 3
Q_END = ATTN_WIDTH
K_END = Q_END + KV_WIDTH
V_END = K_END + KV_WIDTH
POOL_END = V_END + POOL_WIDTH
SU_END = POOL_END + SGU_WIDTH
SV_END = SU_END + SGU_WIDTH
N_IN = SV_END + N_BRANCH * D_MODEL
D_FF = 2816
CONV_W = 3
PAST_LEN = 1024

LANES = 128
SUBLANES = 8
HALF_LANES = LANES // 2
VMEM_LIMIT_BYTES = 56 * 1024 * 1024

NEG_BIG = -1e30
PAIRS = GROUP // 2
KEY_WIN = 4 * CHUNK
PREV_ROWS = KEY_WIN - CHUNK
POOL_PAD = 16
CONV_PAD = SUBLANES
MOD_ROWS = 16
PROJ_SLAB = 2 * 256


def _alibi_slope(head):
    return 2.0 ** (-8.0 * (head + 1) / N_HEADS)


def _rmsnorm(x, g):
    return x * lax.rsqrt(jnp.mean(x * x, axis=-1, keepdims=True) + EPS) * g


def _dot(a, b):
    return jnp.dot(a, b, preferred_element_type=F32)


def _dot_nt(a, b):
    return lax.dot_general(a, b, (((1,), (1,)), ((), ())), preferred_element_type=F32)


def _mod_kernel(c_ref, w_ref, b_ref, o_ref):
    c = c_ref[...]
    sc = c * jax.nn.sigmoid(c)
    o_ref[0] = jnp.dot(sc, w_ref[0], preferred_element_type=F32,
                       precision=lax.Precision.HIGHEST) + b_ref[0]


def _modulation(c_all, w_ada, b_ada):
    depth, d, n = w_ada.shape
    bn = n // 4
    return pl.pallas_call(
        _mod_kernel,
        grid=(depth, n // bn),
        in_specs=[
            pl.BlockSpec((MOD_ROWS, d), lambda l, j: (0, 0)),
            pl.BlockSpec((1, d, bn), lambda l, j: (l, 0, j)),
            pl.BlockSpec((1, 1, bn), lambda l, j: (l, 0, j)),
        ],
        out_specs=pl.BlockSpec((1, MOD_ROWS, bn), lambda l, j: (l, 0, j)),
        out_shape=jax.ShapeDtypeStruct((depth, MOD_ROWS, n), F32),
        compiler_params=pltpu.CompilerParams(
            dimension_semantics=("arbitrary", "arbitrary"),
            vmem_limit_bytes=VMEM_LIMIT_BYTES),
        name="adaln_modulation",
    )(c_all, w_ada, b_ada.reshape(depth, 1, n))


def _split_heads(z):
    lane = lax.broadcasted_iota(jnp.int32, z.shape, 1)
    low = lane < HALF_LANES
    zs = pltpu.roll(z, HALF_LANES, axis=1)
    zero = jnp.zeros_like(z)
    top0 = jnp.where(low, z, zero)
    bot0 = jnp.where(low, zero, zs)
    top1 = jnp.where(low, zs, zero)
    bot1 = jnp.where(low, zero, z)
    return top0, bot0, top1, bot1


def _zero_after(*results):
    bits = None
    for r in results:
        b = lax.bitcast_convert_type(r[0:SUBLANES, 0:LANES], jnp.uint32)
        bits = b if bits is None else bits | b
    bits = lax.shift_right_logical(lax.shift_right_logical(bits, jnp.uint32(16)), jnp.uint32(16))
    return lax.bitcast_convert_type(bits, F32)[0:1, 0:1]


def _dot_tn(a, b):
    return lax.dot_general(a, b, (((0,), (0,)), ((), ())), preferred_element_type=F32)


def _attn_logits(kt, qs):
    return _dot_nt(kt, qs)


def _attn_finish(logits_t, vb, bias_t, sink_a, sink_b):
    w = vb.shape[0] // 2
    logits_t = logits_t + bias_t
    la = logits_t[:w, :]
    lb = logits_t[w:, :]
    ma = jnp.maximum(jnp.max(la, axis=0, keepdims=True), sink_a)
    mb = jnp.maximum(jnp.max(lb, axis=0, keepdims=True), sink_b)
    pa = jnp.exp(la - ma)
    pb = jnp.exp(lb - mb)
    da = jnp.sum(pa, axis=0, keepdims=True) + jnp.exp(sink_a - ma)
    db = jnp.sum(pb, axis=0, keepdims=True) + jnp.exp(sink_b - mb)
    p_t = jnp.concatenate([pa, pb], axis=0).astype(BF16)
    o_t = _dot_tn(vb, p_t)
    row = lax.broadcasted_iota(jnp.int32, o_t.shape, 0)
    o_t = o_t * jnp.where(row < HALF_LANES, 1.0 / da, 1.0 / db)
    return o_t.T


def _pool_window_sums(e):
    outs = []
    for gi, w in enumerate(POOL_WINDOWS):
        s = e[:, gi * POOL_GROUP_WIDTH:(gi + 1) * POOL_GROUP_WIDTH]
        k = 1
        while k < w:
            s = s + pltpu.roll(s, k, axis=0)
            k *= 2
        outs.append(s)
    return outs


def _pool_branch(sums, pu, pos, poolw_ref, pscale):
    outs = []
    for gi, w in enumerate(POOL_WINDOWS):
        c0, c1 = gi * POOL_GROUP_WIDTH, (gi + 1) * POOL_GROUP_WIDTH
        cnt = jnp.minimum(pos + 1, w).astype(F32)
        d = sums[gi] / cnt - pu[:, c0:c1]
        outs.append(_dot(d.astype(BF16), poolw_ref[gi]))
    return jnp.concatenate(outs, axis=1) * pscale


def _sgu_weight(sguw_ref, g, length):
    i = lax.broadcasted_iota(jnp.int32, (length, length), 0)
    j = lax.broadcasted_iota(jnp.int32, (length, length), 1)
    w = sguw_ref[g, :length, :length]
    return jnp.where((j // CHUNK) <= (i // CHUNK), w, jnp.zeros_like(w))


def _merge_and_project(x, g1, zg, ya, yb, yc, woa_ref, wop_ref, wos_ref, wout_ref):
    ga = zg[:, :D_MODEL]
    gb = zg[:, D_MODEL:2 * D_MODEL]
    gc = zg[:, 2 * D_MODEL:]
    merged = (jax.nn.sigmoid(ga) * _dot(ya, woa_ref[...])
              + jax.nn.sigmoid(gb) * _dot(yb.astype(BF16), wop_ref[...])
              + jax.nn.sigmoid(gc) * _dot(yc.astype(BF16), wos_ref[...]))
    return x + g1 * _dot(merged.astype(BF16), wout_ref[...])


def _build_prompt_bias(bias_tab):
    j = lax.broadcasted_iota(jnp.int32, (KEY_WIN, CHUNK), 0)
    q = lax.broadcasted_iota(jnp.int32, (KEY_WIN, CHUNK), 1)
    dist = jnp.abs(PREV_ROWS + q - j).astype(F32)
    for variant, first_valid in enumerate((3 * CHUNK, 2 * CHUNK, CHUNK)):
        valid = j >= first_valid
        for kv in range(N_KV_HEADS):
            for par in range(2):
                blk = jnp.concatenate(
                    [jnp.where(valid, -_alibi_slope(kv * GROUP + 2 * p + par) * dist, NEG_BIG)
                     for p in range(PAIRS)], axis=1)
                bias_tab[variant, kv, par * KEY_WIN:(par + 1) * KEY_WIN, :] = blk


def _tokmix_prompt_kernel(x_ref, mod_ref, n1g_ref, win_ref, woa_ref, wop_ref, wos_ref, wout_ref,
                          poolw_ref, pscale_ref, sgug_ref, sguw_ref, sgub_ref, sink_ref,
                          xo_ref, nk_ref, nv_ref, npool_ref,
                          kext, vext, pext, ya_scr, bias_tab, *, tt):
    b = pl.program_id(0)
    t = pl.program_id(1)
    n_chunks = tt // CHUNK

    @pl.when(jnp.logical_and(b == 0, t == 0))
    def _():
        _build_prompt_bias(bias_tab)

    @pl.when(t == 0)
    def _():
        kext[:, 0:PREV_ROWS, :] = jnp.zeros((4, PREV_ROWS, LANES), BF16)
        vext[:, 0:PREV_ROWS, :] = jnp.zeros((4, PREV_ROWS, LANES), BF16)
        pext[0:POOL_PAD, :] = jnp.zeros((POOL_PAD, POOL_WIDTH), F32)

    x = x_ref[0]
    sh1 = mod_ref[0, :, 0:D_MODEL]
    sc1 = mod_ref[0, :, D_MODEL:2 * D_MODEL]
    g1 = mod_ref[0, :, 2 * D_MODEL:3 * D_MODEL]
    h = _rmsnorm(x, n1g_ref[...]) * (1.0 + sc1) + sh1
    hb = h.astype(BF16)

    zqkv = _dot(hb, win_ref[:, 0:V_END])
    zk = zqkv[:, Q_END:K_END]
    zv = zqkv[:, K_END:V_END]
    nk_ref[0] = zk[tt - WINDOW:, :]
    nv_ref[0] = zv[tt - WINDOW:, :]
    qb = (zqkv[:, 0:Q_END] * (HEAD_DIM ** -0.5)).astype(BF16)
    for idx, part in enumerate(_split_heads(zk)):
        kext[idx, PREV_ROWS:PREV_ROWS + tt, :] = part.astype(BF16)
    for idx, part in enumerate(_split_heads(zv)):
        vext[idx, PREV_ROWS:PREV_ROWS + tt, :] = part.astype(BF16)

    slab_starts = list(range(V_END, N_IN, PROJ_SLAB))
    slabs = []

    def project_slab():
        c0 = slab_starts[len(slabs)]
        slabs.append(_dot(hb, win_ref[:, c0:c0 + PROJ_SLAB]))
        return slabs[-1]

    def pool_projection():
        pu = slabs[0]
        npool_ref[0] = pu[tt - POOL_STATE:, :]
        pext[POOL_PAD:POOL_PAD + tt, :] = pu
        sums = [s[POOL_PAD:, :] for s in _pool_window_sums(pext[...])]
        pos = t * tt + lax.broadcasted_iota(jnp.int32, (tt, POOL_GROUP_WIDTH), 0)
        yb = _pool_branch(sums, pu, pos, poolw_ref, pscale_ref[...])
        pext[0:POOL_PAD, :] = pext[tt:tt + POOL_PAD, :]
        return _dot(yb.astype(BF16), wop_ref[...])

    def block_logits(i, kv, zero=None):
        r0 = i * CHUNK
        qs = jnp.concatenate(
            [qb[r0:r0 + CHUNK, (kv * PAIRS + p) * LANES:(kv * PAIRS + p + 1) * LANES]
             for p in range(PAIRS)], axis=0)
        kt = jnp.concatenate([kext[2 * kv, r0:r0 + KEY_WIN, :],
                              kext[2 * kv + 1, r0:r0 + KEY_WIN, :]], axis=0)
        if zero is not None:
            kt = kt + zero.astype(BF16)
        return _attn_logits(kt, qs)

    def block_finish(i, kv, logits, zero):
        r0 = i * CHUNK
        variant = jnp.where(t == 0, min(i, 2), 2) if i < 2 else 2
        vb = jnp.concatenate([vext[2 * kv, r0:r0 + KEY_WIN, :],
                              vext[2 * kv + 1, r0:r0 + KEY_WIN, :]], axis=0)
        vb = vb + zero.astype(BF16)
        o = _attn_finish(logits, vb, bias_tab[variant, kv], sink_ref[kv, 0], sink_ref[kv, 1])
        for p in range(PAIRS):
            ya_scr[r0:r0 + CHUNK, (kv * PAIRS + p) * LANES:(kv * PAIRS + p + 1) * LANES] = (
                o[p * CHUNK:(p + 1) * CHUNK, :].astype(BF16))

    blocks = [(i, kv) for i in range(n_chunks) for kv in range(N_KV_HEADS)]
    n_lead = len(slab_starts) + 1 - len(blocks)
    assert n_lead >= 1
    for _ in range(n_lead):
        lead = project_slab()
    logits = block_logits(*blocks[0], _zero_after(lead))
    merged_b = None
    for n, blk in enumerate(blocks):
        ahead = []
        nxt = None
        if n + 1 < len(blocks):
            nxt = block_logits(*blocks[n + 1])
            ahead.append(nxt)
        if len(slabs) < len(slab_starts):
            ahead.append(project_slab())
        else:
            merged_b = pool_projection()
            ahead.append(merged_b)
        block_finish(*blk, logits, _zero_after(*ahead))
        logits = nxt

    kext[:, 0:PREV_ROWS, :] = kext[:, tt:tt + PREV_ROWS, :]
    vext[:, 0:PREV_ROWS, :] = vext[:, tt:tt + PREV_ROWS, :]

    zsgu = jnp.concatenate(slabs[1:3], axis=1)
    uu = jax.nn.gelu(zsgu[:, 0:SGU_WIDTH])
    vn = _rmsnorm(jax.nn.gelu(zsgu[:, SGU_WIDTH:]), sgug_ref[...])
    vnb = vn.astype(BF16)
    n_sgu = tt // SGU_LEN
    yc_cols = []
    for g in range(SGU_GROUPS):
        c0, c1 = g * SGU_GROUP_WIDTH, (g + 1) * SGU_GROUP_WIDTH
        rhs = jnp.concatenate([vnb[n * SGU_LEN:(n + 1) * SGU_LEN, c0:c1] for n in range(n_sgu)], axis=1)
        s = _dot(_sgu_weight(sguw_ref, g, SGU_LEN), rhs) + sgub_ref[:, g:g + 1]
        s_rows = jnp.concatenate([s[:, n * SGU_GROUP_WIDTH:(n + 1) * SGU_GROUP_WIDTH]
                                  for n in range(n_sgu)], axis=0)
        yc_cols.append(uu[:, c0:c1] * s_rows)
    yc = jnp.concatenate(yc_cols, axis=1)

    zg = jnp.concatenate(slabs[3:], axis=1)
    merged = (jax.nn.sigmoid(zg[:, :D_MODEL]) * _dot(ya_scr[...], woa_ref[...])
              + jax.nn.sigmoid(zg[:, D_MODEL:2 * D_MODEL]) * merged_b
              + jax.nn.sigmoid(zg[:, 2 * D_MODEL:]) * _dot(yc.astype(BF16), wos_ref[...]))
    xo_ref[0] = x + g1 * _dot(merged.astype(BF16), wout_ref[...])


def _resident(shape, layer=None):
    if layer is None:
        return pl.BlockSpec(shape, lambda *_: (0,) * len(shape), pipeline_mode=pl.Buffered(1))
    return pl.BlockSpec((None,) + tuple(shape), lambda *_: (layer,) + (0,) * len(shape),
                        pipeline_mode=pl.Buffered(1))


def _tokmix_prompt(x, mod, lw, layer, tt):
    bsz, seq, d = x.shape
    nt = seq // tt
    rows = PAIRS * CHUNK
    kernel = functools.partial(_tokmix_prompt_kernel, tt=tt)
    out_shape = (
        jax.ShapeDtypeStruct((bsz, seq, d), F32),
        jax.ShapeDtypeStruct((bsz, WINDOW, KV_WIDTH), F32),
        jax.ShapeDtypeStruct((bsz, WINDOW, KV_WIDTH), F32),
        jax.ShapeDtypeStruct((bsz, POOL_STATE, POOL_WIDTH), F32),
    )
    return pl.pallas_call(
        kernel,
        grid=(bsz, nt),
        in_specs=[
            pl.BlockSpec((1, tt, d), lambda b, t: (b, t, 0)),
            pl.BlockSpec((1, 1, 6 * d), lambda b, t: (b, 0, 0)),
            _resident((1, d), layer),
            _resident((d, N_IN), layer),
            _resident((ATTN_WIDTH, d), layer),
            _resident((POOL_WIDTH, d), layer),
            _resident((SGU_WIDTH, d), layer),
            _resident((d, d), layer),
            _resident((POOL_GROUPS, POOL_GROUP_WIDTH, POOL_GROUP_WIDTH), layer),
            _resident((1, POOL_WIDTH), layer),
            _resident((1, SGU_WIDTH), layer),
            _resident((SGU_GROUPS, SGU_LEN, SGU_LEN), layer),
            _resident((SGU_LEN, SGU_GROUPS), layer),
            _resident((N_KV_HEADS, 2, 1, rows)),
        ],
        out_specs=(
            pl.BlockSpec((1, tt, d), lambda b, t: (b, t, 0)),
            pl.BlockSpec((1, WINDOW, KV_WIDTH), lambda b, t: (b, 0, 0)),
            pl.BlockSpec((1, WINDOW, KV_WIDTH), lambda b, t: (b, 0, 0)),
            pl.BlockSpec((1, POOL_STATE, POOL_WIDTH), lambda b, t: (b, 0, 0)),
        ),
        out_shape=out_shape,
        scratch_shapes=[
            pltpu.VMEM((4, PREV_ROWS + tt, LANES), BF16),
            pltpu.VMEM((4, PREV_ROWS + tt, LANES), BF16),
            pltpu.VMEM((POOL_PAD + tt, POOL_WIDTH), F32),
            pltpu.VMEM((tt, ATTN_WIDTH), BF16),
            pltpu.VMEM((3, N_KV_HEADS, 2 * KEY_WIN, rows), F32),
        ],
        compiler_params=pltpu.CompilerParams(
            dimension_semantics=("arbitrary", "arbitrary"),
            vmem_limit_bytes=VMEM_LIMIT_BYTES),
        name="tokmix_prompt",
    )(x, mod, lw["norm1_g"], lw["w_in"], lw["w_o_attn"], lw["w_o_pool"], lw["w_o_sgu"], lw["w_out"],
      lw["pool_w"], lw["pool_scale"], lw["sgu_norm_g"], lw["sgu_w"], lw["sgu_b_t"],
      _sink_table(lw["attn_sink"][layer], CHUNK))


def _sink_table(sink, rows_per_pair):
    s = sink.astype(F32).reshape(N_KV_HEADS, PAIRS, 2)
    s = jnp.transpose(s, (0, 2, 1))
    s = jnp.repeat(s, rows_per_pair, axis=2)
    return s[:, :, None, :]


def _conv_ffn_tail(x, g2, a, a1, a2, bgate, cw_ref, cb, wdown_ref):
    conv = a2 * cw_ref[0:1, :] + a1 * cw_ref[1:2, :] + a * cw_ref[2:3, :] + cb
    f = _dot((jax.nn.gelu(conv) * bgate).astype(BF16), wdown_ref[...])
    return x + g2 * f


def _channel_prompt_kernel(x_ref, mod_ref, n2g_ref, wup_ref, cw_ref, cb_ref, wdown_ref, fng_ref,
                           xo_ref, nconv_ref, aext, *, tt, final):
    t = pl.program_id(1)

    @pl.when(t == 0)
    def _():
        aext[0:CONV_PAD, :] = jnp.zeros((CONV_PAD, D_FF), F32)

    x = x_ref[0]
    sh2 = mod_ref[0, :, 3 * D_MODEL:4 * D_MODEL]
    sc2 = mod_ref[0, :, 4 * D_MODEL:5 * D_MODEL]
    g2 = mod_ref[0, :, 5 * D_MODEL:6 * D_MODEL]
    h = _rmsnorm(x, n2g_ref[...]) * (1.0 + sc2) + sh2
    up = _dot(h.astype(BF16), wup_ref[...])
    a = up[:, 0:D_FF]
    bgate = up[:, D_FF:]
    nconv_ref[0] = a[tt - (CONV_W - 1):, :]
    aext[CONV_PAD:CONV_PAD + tt, :] = a
    a1 = aext[CONV_PAD - 1:CONV_PAD - 1 + tt, :]
    a2 = aext[CONV_PAD - 2:CONV_PAD - 2 + tt, :]
    y = _conv_ffn_tail(x, g2, a, a1, a2, bgate, cw_ref, cb_ref[...], wdown_ref)
    aext[0:CONV_PAD, :] = aext[tt:tt + CONV_PAD, :]
    if final:
        y = _rmsnorm(y, fng_ref[...])
    xo_ref[0] = y


def _channel_prompt(x, mod, lw, layer, final_g, tt, final):
    bsz, seq, d = x.shape
    nt = seq // tt
    kernel = functools.partial(_channel_prompt_kernel, tt=tt, final=final)
    return pl.pallas_call(
        kernel,
        grid=(bsz, nt),
        in_specs=[
            pl.BlockSpec((1, tt, d), lambda b, t: (b, t, 0)),
            pl.BlockSpec((1, 1, 6 * d), lambda b, t: (b, 0, 0)),
            _resident((1, d), layer),
            _resident((d, 2 * D_FF), layer),
            _resident((CONV_W, D_FF), layer),
            _resident((1, D_FF), layer),
            _resident((D_FF, d), layer),
            _resident((1, d)),
        ],
        out_specs=(
            pl.BlockSpec((1, tt, d), lambda b, t: (b, t, 0)),
            pl.BlockSpec((1, CONV_W - 1, D_FF), lambda b, t: (b, 0, 0)),
        ),
        out_shape=(
            jax.ShapeDtypeStruct((bsz, seq, d), F32),
            jax.ShapeDtypeStruct((bsz, CONV_W - 1, D_FF), F32),
        ),
        scratch_shapes=[pltpu.VMEM((CONV_PAD + tt, D_FF), F32)],
        compiler_params=pltpu.CompilerParams(
            dimension_semantics=("arbitrary", "arbitrary"),
            vmem_limit_bytes=VMEM_LIMIT_BYTES),
        name="channel_prompt",
    )(x, mod, lw["norm2_g"], lw["ffn_w_up"], lw["ffn_conv_w"], lw["ffn_conv_b"], lw["ffn_w_down"],
      final_g)


def _tokmix_sample_kernel(x_ref, mod_ref, ck_ref, cv_ref, sp_ref, n1g_ref, win_ref, woa_ref, wop_ref,
                          wos_ref, wout_ref, poolw_ref, pscale_ref, sgug_ref, sguw_ref, sgub_ref,
                          sink_ref,
                          xo_ref, nk_ref, nv_ref, npool_ref, vn_ref,
                          ya_scr, *, pos0):
    nb, ts, d = x_ref.shape
    wc = ck_ref.shape[1]
    n_keys = wc + ts
    key_win = 2 * LANES

    x3 = x_ref[...]
    sh1 = mod_ref[:, :, 0:D_MODEL]
    sc1 = mod_ref[:, :, D_MODEL:2 * D_MODEL]
    g1 = mod_ref[:, :, 2 * D_MODEL:3 * D_MODEL]
    h3 = _rmsnorm(x3, n1g_ref[...]) * (1.0 + sc1) + sh1
    hb = h3.reshape(nb * ts, d).astype(BF16)

    zqkv = _dot(hb, win_ref[:, 0:V_END])
    qb = (zqkv[:, 0:Q_END] * (HEAD_DIM ** -0.5)).astype(BF16)

    j = lax.broadcasted_iota(jnp.int32, (key_win, ts), 0)
    q = lax.broadcasted_iota(jnp.int32, (key_win, ts), 1)
    dist = jnp.abs(wc + q - j).astype(F32)
    valid = jnp.logical_and(j < n_keys, (pos0 - wc + j) >= 0)
    bias = []
    for kv in range(N_KV_HEADS):
        blocks = []
        for par in range(2):
            row = [jnp.where(valid, -_alibi_slope(kv * GROUP + 2 * p + par) * dist, NEG_BIG)
                   for p in range(PAIRS)]
            blocks.append(jnp.concatenate(row, axis=1))
        bias.append(jnp.concatenate(blocks, axis=0))

    pad = jnp.zeros((key_win - n_keys, KV_WIDTH), F32)
    for b in range(nb):
        r0 = b * ts
        zk = zqkv[r0:r0 + ts, Q_END:K_END]
        zv = zqkv[r0:r0 + ts, K_END:V_END]
        ck = ck_ref[b]
        cv = cv_ref[b]
        nk_ref[b] = jnp.concatenate([ck[ts:, :], zk], axis=0)
        nv_ref[b] = jnp.concatenate([cv[ts:, :], zv], axis=0)
        ksplit = [part.astype(BF16) for part in _split_heads(jnp.concatenate([ck, zk, pad], axis=0))]
        vsplit = [part.astype(BF16) for part in _split_heads(jnp.concatenate([cv, zv, pad], axis=0))]
        for kv in range(N_KV_HEADS):
            qs = jnp.concatenate(
                [qb[r0:r0 + ts, (kv * PAIRS + p) * LANES:(kv * PAIRS + p + 1) * LANES]
                 for p in range(PAIRS)], axis=0)
            kt = jnp.concatenate([ksplit[2 * kv], ksplit[2 * kv + 1]], axis=0)
            vb = jnp.concatenate([vsplit[2 * kv], vsplit[2 * kv + 1]], axis=0)
            o = _attn_finish(_attn_logits(kt, qs), vb, bias[kv], sink_ref[kv, 0], sink_ref[kv, 1])
            for p in range(PAIRS):
                ya_scr[r0:r0 + ts, (kv * PAIRS + p) * LANES:(kv * PAIRS + p + 1) * LANES] = (
                    o[p * ts:(p + 1) * ts, :].astype(BF16))

    zmix = _dot(hb, win_ref[:, V_END:SV_END])
    pu = zmix[:, 0:POOL_WIDTH]
    head = jnp.zeros((POOL_PAD - POOL_STATE, POOL_WIDTH), F32)
    sums = [[] for _ in POOL_WINDOWS]
    for b in range(nb):
        pub = pu[b * ts:(b + 1) * ts, :]
        npool_ref[b] = pub[ts - POOL_STATE:, :]
        e = jnp.concatenate([head, sp_ref[b], pub], axis=0)
        for gi, s in enumerate(_pool_window_sums(e)):
            sums[gi].append(s[POOL_PAD:, :])
    sums = [jnp.concatenate(s, axis=0) for s in sums]
    pos = pos0 + lax.broadcasted_iota(jnp.int32, (nb, ts, POOL_GROUP_WIDTH), 1).reshape(
        nb * ts, POOL_GROUP_WIDTH)
    yb = _pool_branch(sums, pu, pos, poolw_ref, pscale_ref[...])

    uu = jax.nn.gelu(zmix[:, POOL_WIDTH:POOL_WIDTH + SGU_WIDTH])
    vn = _rmsnorm(jax.nn.gelu(zmix[:, POOL_WIDTH + SGU_WIDTH:]), sgug_ref[...])
    vn_ref[...] = vn.reshape(nb, ts, SGU_WIDTH)
    vnb = vn.astype(BF16)
    yc_cols = []
    for g in range(SGU_GROUPS):
        c0, c1 = g * SGU_GROUP_WIDTH, (g + 1) * SGU_GROUP_WIDTH
        rhs = jnp.concatenate([vnb[b * ts:(b + 1) * ts, c0:c1] for b in range(nb)], axis=1)
        s = _dot(_sgu_weight(sguw_ref, g, ts), rhs) + sgub_ref[0:ts, g:g + 1]
        s_rows = jnp.concatenate([s[:, b * SGU_GROUP_WIDTH:(b + 1) * SGU_GROUP_WIDTH]
                                  for b in range(nb)], axis=0)
        yc_cols.append(uu[:, c0:c1] * s_rows)
    yc = jnp.concatenate(yc_cols, axis=1)

    x2 = x3.reshape(nb * ts, d)
    g1r = jnp.broadcast_to(g1, (nb, ts, d)).reshape(nb * ts, d)
    zg = _dot(hb, win_ref[:, SV_END:N_IN])
    y = _merge_and_project(x2, g1r, zg, ya_scr[...], yb, yc, woa_ref, wop_ref, wos_ref, wout_ref)
    xo_ref[...] = y.reshape(nb, ts, d)


def _full(shape, layer=None):
    if layer is None:
        return pl.BlockSpec(shape, lambda *_: (0,) * len(shape))
    return pl.BlockSpec((None,) + tuple(shape[1:]), lambda *_: (layer,) + (0,) * (len(shape) - 1))


def _tokmix_sample(x, mod, cache_k, cache_v, state_pool, lw, layer, pos0):
    nb, ts, d = x.shape
    wc = cache_k.shape[1]
    kernel = functools.partial(_tokmix_sample_kernel, pos0=pos0)
    per_layer = (lw["norm1_g"], lw["w_in"], lw["w_o_attn"], lw["w_o_pool"], lw["w_o_sgu"], lw["w_out"],
                 lw["pool_w"], lw["pool_scale"], lw["sgu_norm_g"], lw["sgu_w"], lw["sgu_b_t"])
    shared = (x, mod, cache_k, cache_v, state_pool)
    sink = _sink_table(lw["attn_sink"][layer], ts)
    args = shared + per_layer + (sink,)
    in_specs = ([_full(a.shape) for a in shared] + [_full(a.shape, layer) for a in per_layer]
                + [_full(sink.shape)])
    out_shape = (
        jax.ShapeDtypeStruct((nb, ts, d), F32),
        jax.ShapeDtypeStruct((nb, wc, KV_WIDTH), F32),
        jax.ShapeDtypeStruct((nb, wc, KV_WIDTH), F32),
        jax.ShapeDtypeStruct((nb, POOL_STATE, POOL_WIDTH), F32),
        jax.ShapeDtypeStruct((nb, ts, SGU_WIDTH), F32),
    )
    return pl.pallas_call(
        kernel,
        grid=(1,),
        in_specs=in_specs,
        out_specs=tuple(_full(s.shape) for s in out_shape),
        out_shape=out_shape,
        scratch_shapes=[pltpu.VMEM((nb * ts, ATTN_WIDTH), BF16)],
        compiler_params=pltpu.CompilerParams(
            dimension_semantics=("arbitrary",),
            vmem_limit_bytes=VMEM_LIMIT_BYTES),
        name="tokmix_sample",
    )(*args)


def _channel_sample_kernel(x_ref, mod_ref, sc_ref, n2g_ref, wup_ref, cw_ref, cb_ref, wdown_ref, fng_ref,
                           xo_ref, nconv_ref, aext, *, final):
    nb, ts, d = x_ref.shape
    x3 = x_ref[...]
    sh2 = mod_ref[:, :, 3 * D_MODEL:4 * D_MODEL]
    sc2 = mod_ref[:, :, 4 * D_MODEL:5 * D_MODEL]
    g2 = mod_ref[:, :, 5 * D_MODEL:6 * D_MODEL]
    h3 = _rmsnorm(x3, n2g_ref[...]) * (1.0 + sc2) + sh2
    up = _dot(h3.reshape(nb * ts, d).astype(BF16), wup_ref[...])
    a = up[:, 0:D_FF]
    bgate = up[:, D_FF:]
    a1s, a2s = [], []
    for b in range(nb):
        ab = a[b * ts:(b + 1) * ts, :]
        nconv_ref[b] = ab[ts - (CONV_W - 1):, :]
        aext[b, CONV_PAD - (CONV_W - 1):CONV_PAD, :] = sc_ref[b]
        aext[b, CONV_PAD:CONV_PAD + ts, :] = ab
        a1s.append(aext[b, CONV_PAD - 1:CONV_PAD - 1 + ts, :])
        a2s.append(aext[b, CONV_PAD - 2:CONV_PAD - 2 + ts, :])
    a1 = jnp.concatenate(a1s, axis=0)
    a2 = jnp.concatenate(a2s, axis=0)
    x2 = x3.reshape(nb * ts, d)
    g2r = jnp.broadcast_to(g2, (nb, ts, d)).reshape(nb * ts, d)
    y = _conv_ffn_tail(x2, g2r, a, a1, a2, bgate, cw_ref, cb_ref[...], wdown_ref)
    if final:
        y = _rmsnorm(y, fng_ref[...])
    xo_ref[...] = y.reshape(nb, ts, d)


def _channel_sample(x, mod, state_conv, lw, layer, final_g, final):
    nb, ts, d = x.shape
    kernel = functools.partial(_channel_sample_kernel, final=final)
    per_layer = (lw["norm2_g"], lw["ffn_w_up"], lw["ffn_conv_w"], lw["ffn_conv_b"], lw["ffn_w_down"])
    shared = (x, mod, state_conv)
    args = shared + per_layer + (final_g,)
    in_specs = ([_full(a.shape) for a in shared] + [_full(a.shape, layer) for a in per_layer]
                + [_full(final_g.shape)])
    out_shape = (
        jax.ShapeDtypeStruct((nb, ts, d), F32),
        jax.ShapeDtypeStruct((nb, CONV_W - 1, D_FF), F32),
    )
    return pl.pallas_call(
        kernel,
        grid=(1,),
        in_specs=in_specs,
        out_specs=tuple(_full(s.shape) for s in out_shape),
        out_shape=out_shape,
        scratch_shapes=[pltpu.VMEM((nb, CONV_PAD + ts, D_FF), F32)],
        compiler_params=pltpu.CompilerParams(
            dimension_semantics=("arbitrary",),
            vmem_limit_bytes=VMEM_LIMIT_BYTES),
        name="channel_sample",
    )(*args)


MIX_TILES = (256, 128)
FFN_TILES = (256, 128)


def _time_tile(seq, candidates):
    for tt in candidates:
        if seq % tt == 0:
            return tt
    raise ValueError(f"prompt length {seq} must be a multiple of {candidates[-1]}")


def kernel(x_prompt, x_sample, c_prompt, c_sample, cache_k_win, cache_v_win, state_pool, state_ffn_conv,
           norm1_g, norm2_g, w_ada, b_ada, w_in, attn_sink, w_o_attn, pool_w, pool_scale, w_o_pool,
           sgu_norm_g, sgu_w, sgu_b, w_o_sgu, w_out, ffn_w_up, ffn_conv_w, ffn_conv_b, ffn_w_down,
           final_norm_g):
    depth = w_in.shape[0]
    bp, seq, d = x_prompt.shape
    bs, ts, _ = x_sample.shape
    wc = cache_k_win.shape[2]
    assert d == D_MODEL and bp + bs <= MOD_ROWS
    assert seq >= WINDOW and ts % SUBLANES == 0 and ts >= POOL_STATE and wc + ts <= 2 * LANES
    tt_mix = _time_tile(seq, MIX_TILES)
    tt_ffn = _time_tile(seq, FFN_TILES)

    c_all = jnp.concatenate(
        [c_prompt, c_sample, jnp.zeros((MOD_ROWS - bp - bs, d), F32)], axis=0)
    mod = _modulation(c_all, w_ada, b_ada)
    mod_p = mod[:, 0:bp].reshape(depth, bp, 1, 6 * d)
    mod_s = mod[:, bp:bp + bs].reshape(depth, bs, 1, 6 * d)

    bf = lambda w: w.astype(BF16)
    params = dict(
        norm1_g=norm1_g.reshape(depth, 1, d), norm2_g=norm2_g.reshape(depth, 1, d),
        w_in=bf(w_in), attn_sink=attn_sink, w_o_attn=bf(w_o_attn), pool_w=bf(pool_w),
        pool_scale=pool_scale.reshape(depth, 1, POOL_WIDTH), w_o_pool=bf(w_o_pool),
        sgu_norm_g=sgu_norm_g.reshape(depth, 1, SGU_WIDTH), sgu_w=bf(sgu_w),
        sgu_b_t=jnp.transpose(sgu_b, (0, 2, 1)), w_o_sgu=bf(w_o_sgu), w_out=bf(w_out),
        ffn_w_up=bf(ffn_w_up), ffn_conv_w=ffn_conv_w,
        ffn_conv_b=ffn_conv_b.reshape(depth, 1, D_FF), ffn_w_down=bf(ffn_w_down))
    final_g = final_norm_g.reshape(1, d)

    ck = cache_k_win.reshape(depth, bs, wc, KV_WIDTH)
    cv = cache_v_win.reshape(depth, bs, wc, KV_WIDTH)

    xp, xs = x_prompt, x_sample
    kp, vp, pp, cp = [], [], [], []
    ks_, vs_, ps_, cs_, ss_ = [], [], [], [], []
    for l in range(depth):
        final = l == depth - 1
        xp, nk, nv, npool = _tokmix_prompt(xp, mod_p[l], params, l, tt_mix)
        xp, nconv = _channel_prompt(xp, mod_p[l], params, l, final_g, tt_ffn, final)
        kp.append(nk); vp.append(nv); pp.append(npool); cp.append(nconv)
        xs, nk, nv, npool, nsv = _tokmix_sample(xs, mod_s[l], ck[l], cv[l], state_pool[l], params, l,
                                                PAST_LEN)
        xs, nconv = _channel_sample(xs, mod_s[l], state_ffn_conv[l], params, l, final_g, final)
        ks_.append(nk); vs_.append(nv); ps_.append(npool); cs_.append(nconv); ss_.append(nsv)

    kv_shape_p = (depth, bp, WINDOW, N_KV_HEADS, HEAD_DIM)
    kv_shape_s = (depth, bs, wc, N_KV_HEADS, HEAD_DIM)
    return (xp, xs,
            jnp.stack(kp).reshape(kv_shape_p), jnp.stack(vp).reshape(kv_shape_p),
            jnp.stack(pp), jnp.stack(cp),
            jnp.stack(ks_).reshape(kv_shape_s), jnp.stack(vs_).reshape(kv_shape_s),
            jnp.stack(ps_), jnp.stack(cs_), jnp.stack(ss_))
```

```python
import functools
import math

import jax
import jax.numpy as jnp
from jax import lax
from jax.experimental import pallas as pl
from jax.experimental.pallas import tpu as pltpu

F32 = jnp.float32
BF16 = jnp.bfloat16

D_MODEL = 1024
CHUNK = 64
EPS = 1e-6
N_HEADS = 16
N_KV_HEADS = 2
GROUP = N_HEADS // N_KV_HEADS
HEAD_DIM = 64
WINDOW = 128
WINDOW_CHUNKS = WINDOW // CHUNK
ATTN_WIDTH = N_HEADS * HEAD_DIM
KV_WIDTH = N_KV_HEADS * HEAD_DIM
POOL_WINDOWS = (2, 4, 8, 16)
POOL_GROUPS = 4
POOL_GROUP_WIDTH = D_MODEL // 8
POOL_WIDTH = POOL_GROUPS * POOL_GROUP_WIDTH
POOL_STATE = 16 - 1
SGU_LEN = 128
SGU_GROUPS = 4
SGU_GROUP_WIDTH = D_MODEL // 8
SGU_WIDTH = SGU_GROUPS * SGU_GROUP_WIDTH
N_BRANCH = 3
Q_END = ATTN_WIDTH
K_END = Q_END + KV_WIDTH
V_END = K_END + KV_WIDTH
POOL_END = V_END + POOL_WIDTH
SU_END = POOL_END + SGU_WIDTH
SV_END = SU_END + SGU_WIDTH
N_IN = SV_END + N_BRANCH * D_MODEL
D_FF = 2816
CONV_W = 3
PAST_LEN = 1024

LANES = 128
SUBLANES = 8
HALF_LANES = LANES // 2
VMEM_LIMIT_BYTES = 56 * 1024 * 1024

NEG_BIG = -1e30
GELU_C1 = -2.0 * math.sqrt(2.0 / math.pi) * math.log2(math.e)
GELU_C3 = GELU_C1 * 0.044715
PAIRS = GROUP // 2
KEY_WIN = 4 * CHUNK
PREV_ROWS = KEY_WIN - CHUNK
POOL_PAD = 16
CONV_PAD = SUBLANES
MOD_ROWS = 16
PROJ_SLAB = 2 * 256


def _alibi_slope(head):
    return 2.0 ** (-8.0 * (head + 1) / N_HEADS)


def _rmsnorm(x, g):
    return x * lax.rsqrt(jnp.mean(x * x, axis=-1, keepdims=True) + EPS) * g


def _gelu(x):
    return x / (1.0 + jnp.exp2(x * (GELU_C1 + GELU_C3 * (x * x))))


def _dot(a, b):
    return jnp.dot(a, b, preferred_element_type=F32)


def _dot_nt(a, b):
    return lax.dot_general(a, b, (((1,), (1,)), ((), ())), preferred_element_type=F32)


def _mod_kernel(c_ref, w_ref, b_ref, o_ref):
    c = c_ref[...]
    sc = c * jax.nn.sigmoid(c)
    o_ref[0] = jnp.dot(sc, w_ref[0], preferred_element_type=F32,
                       precision=lax.Precision.HIGHEST) + b_ref[0]


def _modulation(c_all, w_ada, b_ada):
    depth, d, n = w_ada.shape
    bn = n // 4
    return pl.pallas_call(
        _mod_kernel,
        grid=(depth, n // bn),
        in_specs=[
            pl.BlockSpec((MOD_ROWS, d), lambda l, j: (0, 0)),
            pl.BlockSpec((1, d, bn), lambda l, j: (l, 0, j)),
            pl.BlockSpec((1, 1, bn), lambda l, j: (l, 0, j)),
        ],
        out_specs=pl.BlockSpec((1, MOD_ROWS, bn), lambda l, j: (l, 0, j)),
        out_shape=jax.ShapeDtypeStruct((depth, MOD_ROWS, n), F32),
        compiler_params=pltpu.CompilerParams(
            dimension_semantics=("arbitrary", "arbitrary"),
            vmem_limit_bytes=VMEM_LIMIT_BYTES),
        name="adaln_modulation",
    )(c_all, w_ada, b_ada.reshape(depth, 1, n))


def _pool_fold_kernel(pw_ref, ps_ref, wo_ref, o_ref):
    for g in range(POOL_GROUPS):
        r0, r1 = g * POOL_GROUP_WIDTH, (g + 1) * POOL_GROUP_WIDTH
        scaled = pw_ref[g] * ps_ref[:, r0:r1]
        o_ref[r0:r1, :] = jnp.dot(scaled, wo_ref[r0:r1, :], preferred_element_type=F32,
                                  precision=lax.Precision.HIGHEST).astype(BF16)


def _fold_pool_weights(pool_w, pool_scale, w_o_pool):
    depth, _, d = w_o_pool.shape
    return pl.pallas_call(
        _pool_fold_kernel,
        grid=(depth,),
        in_specs=[
            pl.BlockSpec((None, POOL_GROUPS, POOL_GROUP_WIDTH, POOL_GROUP_WIDTH), lambda l: (l, 0, 0, 0)),
            pl.BlockSpec((None, 1, POOL_WIDTH), lambda l: (l, 0, 0)),
            pl.BlockSpec((None, POOL_WIDTH, d), lambda l: (l, 0, 0)),
        ],
        out_specs=pl.BlockSpec((None, POOL_WIDTH, d), lambda l: (l, 0, 0)),
        out_shape=jax.ShapeDtypeStruct((depth, POOL_WIDTH, d), BF16),
        compiler_params=pltpu.CompilerParams(
            dimension_semantics=("arbitrary",),
            vmem_limit_bytes=VMEM_LIMIT_BYTES),
        name="pool_weight_fold",
    )(pool_w, pool_scale.reshape(depth, 1, POOL_WIDTH), w_o_pool)


def _split_heads(z):
    lane = lax.broadcasted_iota(jnp.int32, z.shape, 1)
    low = lane < HALF_LANES
    zs = pltpu.roll(z, HALF_LANES, axis=1)
    zero = jnp.zeros_like(z)
    top0 = jnp.where(low, z, zero)
    bot0 = jnp.where(low, zero, zs)
    top1 = jnp.where(low, zs, zero)
    bot1 = jnp.where(low, zero, z)
    return top0, bot0, top1, bot1


def _zero_after(*results):
    bits = None
    for r in results:
        b = lax.bitcast_convert_type(r[0:SUBLANES, 0:LANES], jnp.uint32)
        bits = b if bits is None else bits | b
    bits = lax.shift_right_logical(lax.shift_right_logical(bits, jnp.uint32(16)), jnp.uint32(16))
    return lax.bitcast_convert_type(bits, F32)[0:1, 0:1]


def _attn_logits(kt, qs):
    return _dot_nt(kt, qs)


def _attn_softmax(logits_t, bias_t, sink_a, sink_b):
    w = logits_t.shape[0] // 2
    logits_t = logits_t + bias_t
    la = logits_t[:w, :]
    lb = logits_t[w:, :]
    ma = jnp.maximum(jnp.max(la, axis=0, keepdims=True), sink_a)
    mb = jnp.maximum(jnp.max(lb, axis=0, keepdims=True), sink_b)
    pa = jnp.exp(la - ma)
    pb = jnp.exp(lb - mb)
    da = jnp.sum(pa, axis=0, keepdims=True) + jnp.exp(sink_a - ma)
    db = jnp.sum(pb, axis=0, keepdims=True) + jnp.exp(sink_b - mb)
    return jnp.concatenate([pa, pb], axis=0).astype(BF16), da, db


def _attn_values(p_t, da, db, vb_t):
    o_t = _dot(vb_t, p_t)
    row = lax.broadcasted_iota(jnp.int32, o_t.shape, 0)
    o_t = o_t * jnp.where(row < HALF_LANES, 1.0 / da, 1.0 / db)
    return o_t.T


def _pool_window_sums(e):
    outs = []
    for gi, w in enumerate(POOL_WINDOWS):
        s = e[:, gi * POOL_GROUP_WIDTH:(gi + 1) * POOL_GROUP_WIDTH]
        k = 1
        while k < w:
            s = s + pltpu.roll(s, k, axis=0)
            k *= 2
        outs.append(s)
    return outs


def _pool_deltas(sums, pu, pos):
    outs = []
    for gi, w in enumerate(POOL_WINDOWS):
        c0, c1 = gi * POOL_GROUP_WIDTH, (gi + 1) * POOL_GROUP_WIDTH
        cnt = jnp.minimum(pos + 1, w).astype(F32)
        outs.append(sums[gi] / cnt - pu[:, c0:c1])
    return jnp.concatenate(outs, axis=1)


def _sgu_weight(sguw_ref, g, length):
    i = lax.broadcasted_iota(jnp.int32, (length, length), 0)
    j = lax.broadcasted_iota(jnp.int32, (length, length), 1)
    w = sguw_ref[g, :length, :length]
    return jnp.where((j // CHUNK) <= (i // CHUNK), w, jnp.zeros_like(w))


def _merge_and_project(x, g1, zg, ya, pd, yc, woa_ref, wpool_ref, wos_ref, wout_ref):
    ga = zg[:, :D_MODEL]
    gb = zg[:, D_MODEL:2 * D_MODEL]
    gc = zg[:, 2 * D_MODEL:]
    merged = (jax.nn.sigmoid(ga) * _dot(ya, woa_ref[...])
              + jax.nn.sigmoid(gb) * _dot(pd.astype(BF16), wpool_ref[...])
              + jax.nn.sigmoid(gc) * _dot(yc.astype(BF16), wos_ref[...]))
    return x + g1 * _dot(merged.astype(BF16), wout_ref[...])


def _build_prompt_bias(bias_tab):
    j = lax.broadcasted_iota(jnp.int32, (KEY_WIN, CHUNK), 0)
    q = lax.broadcasted_iota(jnp.int32, (KEY_WIN, CHUNK), 1)
    dist = jnp.abs(PREV_ROWS + q - j).astype(F32)
    for variant, first_valid in enumerate((3 * CHUNK, 2 * CHUNK, CHUNK)):
        valid = j >= first_valid
        for kv in range(N_KV_HEADS):
            for par in range(2):
                blk = jnp.concatenate(
                    [jnp.where(valid, -_alibi_slope(kv * GROUP + 2 * p + par) * dist, NEG_BIG)
                     for p in range(PAIRS)], axis=1)
                bias_tab[variant, kv, par * KEY_WIN:(par + 1) * KEY_WIN, :] = blk


def _tokmix_prompt_kernel(x_ref, mod_ref, n1g_ref, win_ref, woa_ref, wpool_ref, wos_ref, wout_ref,
                          sgug_ref, sguw_ref, sgub_ref, sink_ref,
                          xo_ref, nk_ref, nv_ref, npool_ref,
                          kext, vext, pext, ya_scr, bias_tab, *, tt):
    b = pl.program_id(0)
    t = pl.program_id(1)
    n_chunks = tt // CHUNK

    @pl.when(jnp.logical_and(b == 0, t == 0))
    def _():
        _build_prompt_bias(bias_tab)

    @pl.when(t == 0)
    def _():
        kext[:, 0:PREV_ROWS, :] = jnp.zeros((4, PREV_ROWS, LANES), BF16)
        vext[:, 0:PREV_ROWS, :] = jnp.zeros((4, PREV_ROWS, LANES), BF16)
        pext[0:POOL_PAD, :] = jnp.zeros((POOL_PAD, POOL_WIDTH), F32)

    x = x_ref[0]
    sh1 = mod_ref[0, :, 0:D_MODEL]
    sc1 = mod_ref[0, :, D_MODEL:2 * D_MODEL]
    g1 = mod_ref[0, :, 2 * D_MODEL:3 * D_MODEL]
    h = _rmsnorm(x, n1g_ref[...]) * (1.0 + sc1) + sh1
    hb = h.astype(BF16)

    zqkv = _dot(hb, win_ref[:, 0:V_END])
    zk = zqkv[:, Q_END:K_END]
    zv = zqkv[:, K_END:V_END]
    nk_ref[0] = zk[tt - WINDOW:, :]
    nv_ref[0] = zv[tt - WINDOW:, :]
    qb = (zqkv[:, 0:Q_END] * (HEAD_DIM ** -0.5)).astype(BF16)
    for idx, part in enumerate(_split_heads(zk)):
        kext[idx, PREV_ROWS:PREV_ROWS + tt, :] = part.astype(BF16)
    for idx, part in enumerate(_split_heads(zv)):
        vext[idx, PREV_ROWS:PREV_ROWS + tt, :] = part.astype(BF16)

    slab_starts = list(range(V_END, N_IN, PROJ_SLAB))
    slabs = []

    def project_slab():
        c0 = slab_starts[len(slabs)]
        slabs.append(_dot(hb, win_ref[:, c0:c0 + PROJ_SLAB]))
        return slabs[-1]

    def pool_projection():
        pu = slabs[0]
        npool_ref[0] = pu[tt - POOL_STATE:, :]
        pext[POOL_PAD:POOL_PAD + tt, :] = pu
        sums = [s[POOL_PAD:, :] for s in _pool_window_sums(pext[...])]
        pos = t * tt + lax.broadcasted_iota(jnp.int32, (tt, POOL_GROUP_WIDTH), 0)
        pd = _pool_deltas(sums, pu, pos)
        pext[0:POOL_PAD, :] = pext[tt:tt + POOL_PAD, :]
        return _dot(pd.astype(BF16), wpool_ref[...])

    def block_logits(i, kv, zero=None):
        r0 = i * CHUNK
        qs = jnp.concatenate(
            [qb[r0:r0 + CHUNK, (kv * PAIRS + p) * LANES:(kv * PAIRS + p + 1) * LANES]
             for p in range(PAIRS)], axis=0)
        kt = jnp.concatenate([kext[2 * kv, r0:r0 + KEY_WIN, :],
                              kext[2 * kv + 1, r0:r0 + KEY_WIN, :]], axis=0)
        if zero is not None:
            kt = kt + zero.astype(BF16)
        return _attn_logits(kt, qs)

    def block_softmax(i, kv, logits):
        variant = jnp.where(t == 0, min(i, 2), 2) if i < 2 else 2
        return _attn_softmax(logits, bias_tab[variant, kv], sink_ref[kv, 0], sink_ref[kv, 1])

    def block_values(i, kv, p_t, da, db, zero):
        r0 = i * CHUNK
        vb = jnp.concatenate([vext[2 * kv, r0:r0 + KEY_WIN, :],
                              vext[2 * kv + 1, r0:r0 + KEY_WIN, :]], axis=0)
        o = _attn_values(p_t, da, db, vb.T + zero.astype(BF16))
        for p in range(PAIRS):
            ya_scr[r0:r0 + CHUNK, (kv * PAIRS + p) * LANES:(kv * PAIRS + p + 1) * LANES] = (
                o[p * CHUNK:(p + 1) * CHUNK, :].astype(BF16))

    blocks = [(i, kv) for i in range(n_chunks) for kv in range(N_KV_HEADS)]
    n_lead = len(slab_starts) + 1 - len(blocks)
    assert n_lead >= 1
    for _ in range(n_lead):
        lead = project_slab()
    logits = block_logits(*blocks[0], _zero_after(lead))
    merged_b = None
    for n, blk in enumerate(blocks):
        p_t, da, db = block_softmax(*blk, logits)
        ahead = []
        logits = None
        if n + 1 < len(blocks):
            logits = block_logits(*blocks[n + 1])
            ahead.append(logits)
        if len(slabs) < len(slab_starts):
            ahead.append(project_slab())
        else:
            merged_b = pool_projection()
            ahead.append(merged_b)
        block_values(*blk, p_t, da, db, _zero_after(*ahead))

    kext[:, 0:PREV_ROWS, :] = kext[:, tt:tt + PREV_ROWS, :]
    vext[:, 0:PREV_ROWS, :] = vext[:, tt:tt + PREV_ROWS, :]

    zsgu = jnp.concatenate(slabs[1:3], axis=1)
    uu = _gelu(zsgu[:, 0:SGU_WIDTH])
    vn = _rmsnorm(_gelu(zsgu[:, SGU_WIDTH:]), sgug_ref[...])
    vnb = vn.astype(BF16)
    n_sgu = tt // SGU_LEN
    yc_cols = []
    for g in range(SGU_GROUPS):
        c0, c1 = g * SGU_GROUP_WIDTH, (g + 1) * SGU_GROUP_WIDTH
        rhs = jnp.concatenate([vnb[n * SGU_LEN:(n + 1) * SGU_LEN, c0:c1] for n in range(n_sgu)], axis=1)
        s = _dot(_sgu_weight(sguw_ref, g, SGU_LEN), rhs) + sgub_ref[:, g:g + 1]
        s_rows = jnp.concatenate([s[:, n * SGU_GROUP_WIDTH:(n + 1) * SGU_GROUP_WIDTH]
                                  for n in range(n_sgu)], axis=0)
        yc_cols.append(uu[:, c0:c1] * s_rows)
    yc = jnp.concatenate(yc_cols, axis=1)

    zg = jnp.concatenate(slabs[3:], axis=1)
    merged = (jax.nn.sigmoid(zg[:, :D_MODEL]) * _dot(ya_scr[...], woa_ref[...])
              + jax.nn.sigmoid(zg[:, D_MODEL:2 * D_MODEL]) * merged_b
              + jax.nn.sigmoid(zg[:, 2 * D_MODEL:]) * _dot(yc.astype(BF16), wos_ref[...]))
    xo_ref[0] = x + g1 * _dot(merged.astype(BF16), wout_ref[...])


def _resident(shape, layer=None):
    if layer is None:
        return pl.BlockSpec(shape, lambda *_: (0,) * len(shape), pipeline_mode=pl.Buffered(1))
    return pl.BlockSpec((None,) + tuple(shape), lambda *_: (layer,) + (0,) * len(shape),
                        pipeline_mode=pl.Buffered(1))


def _tokmix_prompt(x, mod, lw, layer, tt):
    bsz, seq, d = x.shape
    nt = seq // tt
    rows = PAIRS * CHUNK
    kernel = functools.partial(_tokmix_prompt_kernel, tt=tt)
    out_shape = (
        jax.ShapeDtypeStruct((bsz, seq, d), F32),
        jax.ShapeDtypeStruct((bsz, WINDOW, KV_WIDTH), F32),
        jax.ShapeDtypeStruct((bsz, WINDOW, KV_WIDTH), F32),
        jax.ShapeDtypeStruct((bsz, POOL_STATE, POOL_WIDTH), F32),
    )
    return pl.pallas_call(
        kernel,
        grid=(bsz, nt),
        in_specs=[
            pl.BlockSpec((1, tt, d), lambda b, t: (b, t, 0)),
            pl.BlockSpec((1, 1, 6 * d), lambda b, t: (b, 0, 0)),
            _resident((1, d), layer),
            _resident((d, N_IN), layer),
            _resident((ATTN_WIDTH, d), layer),
            _resident((POOL_WIDTH, d), layer),
            _resident((SGU_WIDTH, d), layer),
            _resident((d, d), layer),
            _resident((1, SGU_WIDTH), layer),
            _resident((SGU_GROUPS, SGU_LEN, SGU_LEN), layer),
            _resident((SGU_LEN, SGU_GROUPS), layer),
            _resident((N_KV_HEADS, 2, 1, rows)),
        ],
        out_specs=(
            pl.BlockSpec((1, tt, d), lambda b, t: (b, t, 0)),
            pl.BlockSpec((1, WINDOW, KV_WIDTH), lambda b, t: (b, 0, 0)),
            pl.BlockSpec((1, WINDOW, KV_WIDTH), lambda b, t: (b, 0, 0)),
            pl.BlockSpec((1, POOL_STATE, POOL_WIDTH), lambda b, t: (b, 0, 0)),
        ),
        out_shape=out_shape,
        scratch_shapes=[
            pltpu.VMEM((4, PREV_ROWS + tt, LANES), BF16),
            pltpu.VMEM((4, PREV_ROWS + tt, LANES), BF16),
            pltpu.VMEM((POOL_PAD + tt, POOL_WIDTH), F32),
            pltpu.VMEM((tt, ATTN_WIDTH), BF16),
            pltpu.VMEM((3, N_KV_HEADS, 2 * KEY_WIN, rows), F32),
        ],
        compiler_params=pltpu.CompilerParams(
            dimension_semantics=("arbitrary", "arbitrary"),
            vmem_limit_bytes=VMEM_LIMIT_BYTES),
        name="tokmix_prompt",
    )(x, mod, lw["norm1_g"], lw["w_in"], lw["w_o_attn"], lw["w_pool"], lw["w_o_sgu"], lw["w_out"],
      lw["sgu_norm_g"], lw["sgu_w"], lw["sgu_b_t"],
      _sink_table(lw["attn_sink"][layer], CHUNK))


def _sink_table(sink, rows_per_pair):
    s = sink.astype(F32).reshape(N_KV_HEADS, PAIRS, 2)
    s = jnp.transpose(s, (0, 2, 1))
    s = jnp.repeat(s, rows_per_pair, axis=2)
    return s[:, :, None, :]


def _conv_ffn_tail(x, g2, a, a1, a2, bgate, cw_ref, cb, wdown_ref):
    conv = a2 * cw_ref[0:1, :] + a1 * cw_ref[1:2, :] + a * cw_ref[2:3, :] + cb
    f = _dot((_gelu(conv) * bgate).astype(BF16), wdown_ref[...])
    return x + g2 * f


def _channel_prompt_kernel(x_ref, mod_ref, n2g_ref, wup_ref, cw_ref, cb_ref, wdown_ref, fng_ref,
                           xo_ref, nconv_ref, aext, *, tt, final):
    t = pl.program_id(1)

    @pl.when(t == 0)
    def _():
        aext[0:CONV_PAD, :] = jnp.zeros((CONV_PAD, D_FF), F32)

    x = x_ref[0]
    sh2 = mod_ref[0, :, 3 * D_MODEL:4 * D_MODEL]
    sc2 = mod_ref[0, :, 4 * D_MODEL:5 * D_MODEL]
    g2 = mod_ref[0, :, 5 * D_MODEL:6 * D_MODEL]
    h = _rmsnorm(x, n2g_ref[...]) * (1.0 + sc2) + sh2
    up = _dot(h.astype(BF16), wup_ref[...])
    a = up[:, 0:D_FF]
    bgate = up[:, D_FF:]
    nconv_ref[0] = a[tt - (CONV_W - 1):, :]
    aext[CONV_PAD:CONV_PAD + tt, :] = a
    a1 = aext[CONV_PAD - 1:CONV_PAD - 1 + tt, :]
    a2 = aext[CONV_PAD - 2:CONV_PAD - 2 + tt, :]
    y = _conv_ffn_tail(x, g2, a, a1, a2, bgate, cw_ref, cb_ref[...], wdown_ref)
    aext[0:CONV_PAD, :] = aext[tt:tt + CONV_PAD, :]
    if final:
        y = _rmsnorm(y, fng_ref[...])
    xo_ref[0] = y


def _channel_prompt(x, mod, lw, layer, final_g, tt, final):
    bsz, seq, d = x.shape
    nt = seq // tt
    kernel = functools.partial(_channel_prompt_kernel, tt=tt, final=final)
    return pl.pallas_call(
        kernel,
        grid=(bsz, nt),
        in_specs=[
            pl.BlockSpec((1, tt, d), lambda b, t: (b, t, 0)),
            pl.BlockSpec((1, 1, 6 * d), lambda b, t: (b, 0, 0)),
            _resident((1, d), layer),
            _resident((d, 2 * D_FF), layer),
            _resident((CONV_W, D_FF), layer),
            _resident((1, D_FF), layer),
            _resident((D_FF, d), layer),
            _resident((1, d)),
        ],
        out_specs=(
            pl.BlockSpec((1, tt, d), lambda b, t: (b, t, 0)),
            pl.BlockSpec((1, CONV_W - 1, D_FF), lambda b, t: (b, 0, 0)),
        ),
        out_shape=(
            jax.ShapeDtypeStruct((bsz, seq, d), F32),
            jax.ShapeDtypeStruct((bsz, CONV_W - 1, D_FF), F32),
        ),
        scratch_shapes=[pltpu.VMEM((CONV_PAD + tt, D_FF), F32)],
        compiler_params=pltpu.CompilerParams(
            dimension_semantics=("arbitrary", "arbitrary"),
            vmem_limit_bytes=VMEM_LIMIT_BYTES),
        name="channel_prompt",
    )(x, mod, lw["norm2_g"], lw["ffn_w_up"], lw["ffn_conv_w"], lw["ffn_conv_b"], lw["ffn_w_down"],
      final_g)


def _tokmix_sample_kernel(x_ref, mod_ref, ck_ref, cv_ref, sp_ref, n1g_ref, win_ref, woa_ref, wpool_ref,
                          wos_ref, wout_ref, sgug_ref, sguw_ref, sgub_ref,
                          sink_ref,
                          xo_ref, nk_ref, nv_ref, npool_ref, vn_ref,
                          ya_scr, *, pos0):
    nb, ts, d = x_ref.shape
    wc = ck_ref.shape[1]
    n_keys = wc + ts
    key_win = 2 * LANES

    x3 = x_ref[...]
    sh1 = mod_ref[:, :, 0:D_MODEL]
    sc1 = mod_ref[:, :, D_MODEL:2 * D_MODEL]
    g1 = mod_ref[:, :, 2 * D_MODEL:3 * D_MODEL]
    h3 = _rmsnorm(x3, n1g_ref[...]) * (1.0 + sc1) + sh1
    hb = h3.reshape(nb * ts, d).astype(BF16)

    zqkv = _dot(hb, win_ref[:, 0:V_END])
    qb = (zqkv[:, 0:Q_END] * (HEAD_DIM ** -0.5)).astype(BF16)

    j = lax.broadcasted_iota(jnp.int32, (key_win, ts), 0)
    q = lax.broadcasted_iota(jnp.int32, (key_win, ts), 1)
    dist = jnp.abs(wc + q - j).astype(F32)
    valid = jnp.logical_and(j < n_keys, (pos0 - wc + j) >= 0)
    bias = []
    for kv in range(N_KV_HEADS):
        blocks = []
        for par in range(2):
            row = [jnp.where(valid, -_alibi_slope(kv * GROUP + 2 * p + par) * dist, NEG_BIG)
                   for p in range(PAIRS)]
            blocks.append(jnp.concatenate(row, axis=1))
        bias.append(jnp.concatenate(blocks, axis=0))

    pad = jnp.zeros((key_win - n_keys, KV_WIDTH), F32)
    for b in range(nb):
        r0 = b * ts
        zk = zqkv[r0:r0 + ts, Q_END:K_END]
        zv = zqkv[r0:r0 + ts, K_END:V_END]
        ck = ck_ref[b]
        cv = cv_ref[b]
        nk_ref[b] = jnp.concatenate([ck[ts:, :], zk], axis=0)
        nv_ref[b] = jnp.concatenate([cv[ts:, :], zv], axis=0)
        ksplit = [part.astype(BF16) for part in _split_heads(jnp.concatenate([ck, zk, pad], axis=0))]
        vsplit = [part.astype(BF16) for part in _split_heads(jnp.concatenate([cv, zv, pad], axis=0))]
        for kv in range(N_KV_HEADS):
            qs = jnp.concatenate(
                [qb[r0:r0 + ts, (kv * PAIRS + p) * LANES:(kv * PAIRS + p + 1) * LANES]
                 for p in range(PAIRS)], axis=0)
            kt = jnp.concatenate([ksplit[2 * kv], ksplit[2 * kv + 1]], axis=0)
            vb = jnp.concatenate([vsplit[2 * kv], vsplit[2 * kv + 1]], axis=0)
            p_t, da, db = _attn_softmax(_attn_logits(kt, qs), bias[kv], sink_ref[kv, 0], sink_ref[kv, 1])
            o = _attn_values(p_t, da, db, vb.T)
            for p in range(PAIRS):
                ya_scr[r0:r0 + ts, (kv * PAIRS + p) * LANES:(kv * PAIRS + p + 1) * LANES] = (
                    o[p * ts:(p + 1) * ts, :].astype(BF16))

    zmix = _dot(hb, win_ref[:, V_END:SV_END])
    pu = zmix[:, 0:POOL_WIDTH]
    head = jnp.zeros((POOL_PAD - POOL_STATE, POOL_WIDTH), F32)
    sums = [[] for _ in POOL_WINDOWS]
    for b in range(nb):
        pub = pu[b * ts:(b + 1) * ts, :]
        npool_ref[b] = pub[ts - POOL_STATE:, :]
        e = jnp.concatenate([head, sp_ref[b], pub], axis=0)
        for gi, s in enumerate(_pool_window_sums(e)):
            sums[gi].append(s[POOL_PAD:, :])
    sums = [jnp.concatenate(s, axis=0) for s in sums]
    pos = pos0 + lax.broadcasted_iota(jnp.int32, (nb, ts, POOL_GROUP_WIDTH), 1).reshape(
        nb * ts, POOL_GROUP_WIDTH)
    pd = _pool_deltas(sums, pu, pos)

    uu = _gelu(zmix[:, POOL_WIDTH:POOL_WIDTH + SGU_WIDTH])
    vn = _rmsnorm(_gelu(zmix[:, POOL_WIDTH + SGU_WIDTH:]), sgug_ref[...])
    vn_ref[...] = vn.reshape(nb, ts, SGU_WIDTH)
    vnb = vn.astype(BF16)
    yc_cols = []
    for g in range(SGU_GROUPS):
        c0, c1 = g * SGU_GROUP_WIDTH, (g + 1) * SGU_GROUP_WIDTH
        rhs = jnp.concatenate([vnb[b * ts:(b + 1) * ts, c0:c1] for b in range(nb)], axis=1)
        s = _dot(_sgu_weight(sguw_ref, g, ts), rhs) + sgub_ref[0:ts, g:g + 1]
        s_rows = jnp.concatenate([s[:, b * SGU_GROUP_WIDTH:(b + 1) * SGU_GROUP_WIDTH]
                                  for b in range(nb)], axis=0)
        yc_cols.append(uu[:, c0:c1] * s_rows)
    yc = jnp.concatenate(yc_cols, axis=1)

    x2 = x3.reshape(nb * ts, d)
    g1r = jnp.broadcast_to(g1, (nb, ts, d)).reshape(nb * ts, d)
    zg = _dot(hb, win_ref[:, SV_END:N_IN])
    y = _merge_and_project(x2, g1r, zg, ya_scr[...], pd, yc, woa_ref, wpool_ref, wos_ref, wout_ref)
    xo_ref[...] = y.reshape(nb, ts, d)


def _full(shape, layer=None):
    if layer is None:
        return pl.BlockSpec(shape, lambda *_: (0,) * len(shape))
    return pl.BlockSpec((None,) + tuple(shape[1:]), lambda *_: (layer,) + (0,) * (len(shape) - 1))


def _tokmix_sample(x, mod, cache_k, cache_v, state_pool, lw, layer, pos0):
    nb, ts, d = x.shape
    wc = cache_k.shape[1]
    kernel = functools.partial(_tokmix_sample_kernel, pos0=pos0)
    per_layer = (lw["norm1_g"], lw["w_in"], lw["w_o_attn"], lw["w_pool"], lw["w_o_sgu"], lw["w_out"],
                 lw["sgu_norm_g"], lw["sgu_w"], lw["sgu_b_t"])
    shared = (x, mod, cache_k, cache_v, state_pool)
    sink = _sink_table(lw["attn_sink"][layer], ts)
    args = shared + per_layer + (sink,)
    in_specs = ([_full(a.shape) for a in shared] + [_full(a.shape, layer) for a in per_layer]
                + [_full(sink.shape)])
    out_shape = (
        jax.ShapeDtypeStruct((nb, ts, d), F32),
        jax.ShapeDtypeStruct((nb, wc, KV_WIDTH), F32),
        jax.ShapeDtypeStruct((nb, wc, KV_WIDTH), F32),
        jax.ShapeDtypeStruct((nb, POOL_STATE, POOL_WIDTH), F32),
        jax.ShapeDtypeStruct((nb, ts, SGU_WIDTH), F32),
    )
    return pl.pallas_call(
        kernel,
        grid=(1,),
        in_specs=in_specs,
        out_specs=tuple(_full(s.shape) for s in out_shape),
        out_shape=out_shape,
        scratch_shapes=[pltpu.VMEM((nb * ts, ATTN_WIDTH), BF16)],
        compiler_params=pltpu.CompilerParams(
            dimension_semantics=("arbitrary",),
            vmem_limit_bytes=VMEM_LIMIT_BYTES),
        name="tokmix_sample",
    )(*args)


def _channel_sample_kernel(x_ref, mod_ref, sc_ref, n2g_ref, wup_ref, cw_ref, cb_ref, wdown_ref, fng_ref,
                           xo_ref, nconv_ref, aext, *, final):
    nb, ts, d = x_ref.shape
    x3 = x_ref[...]
    sh2 = mod_ref[:, :, 3 * D_MODEL:4 * D_MODEL]
    sc2 = mod_ref[:, :, 4 * D_MODEL:5 * D_MODEL]
    g2 = mod_ref[:, :, 5 * D_MODEL:6 * D_MODEL]
    h3 = _rmsnorm(x3, n2g_ref[...]) * (1.0 + sc2) + sh2
    up = _dot(h3.reshape(nb * ts, d).astype(BF16), wup_ref[...])
    a = up[:, 0:D_FF]
    bgate = up[:, D_FF:]
    a1s, a2s = [], []
    for b in range(nb):
        ab = a[b * ts:(b + 1) * ts, :]
        nconv_ref[b] = ab[ts - (CONV_W - 1):, :]
        aext[b, CONV_PAD - (CONV_W - 1):CONV_PAD, :] = sc_ref[b]
        aext[b, CONV_PAD:CONV_PAD + ts, :] = ab
        a1s.append(aext[b, CONV_PAD - 1:CONV_PAD - 1 + ts, :])
        a2s.append(aext[b, CONV_PAD - 2:CONV_PAD - 2 + ts, :])
    a1 = jnp.concatenate(a1s, axis=0)
    a2 = jnp.concatenate(a2s, axis=0)
    x2 = x3.reshape(nb * ts, d)
    g2r = jnp.broadcast_to(g2, (nb, ts, d)).reshape(nb * ts, d)
    y = _conv_ffn_tail(x2, g2r, a, a1, a2, bgate, cw_ref, cb_ref[...], wdown_ref)
    if final:
        y = _rmsnorm(y, fng_ref[...])
    xo_ref[...] = y.reshape(nb, ts, d)


def _channel_sample(x, mod, state_conv, lw, layer, final_g, final):
    nb, ts, d = x.shape
    kernel = functools.partial(_channel_sample_kernel, final=final)
    per_layer = (lw["norm2_g"], lw["ffn_w_up"], lw["ffn_conv_w"], lw["ffn_conv_b"], lw["ffn_w_down"])
    shared = (x, mod, state_conv)
    args = shared + per_layer + (final_g,)
    in_specs = ([_full(a.shape) for a in shared] + [_full(a.shape, layer) for a in per_layer]
                + [_full(final_g.shape)])
    out_shape = (
        jax.ShapeDtypeStruct((nb, ts, d), F32),
        jax.ShapeDtypeStruct((nb, CONV_W - 1, D_FF), F32),
    )
    return pl.pallas_call(
        kernel,
        grid=(1,),
        in_specs=in_specs,
        out_specs=tuple(_full(s.shape) for s in out_shape),
        out_shape=out_shape,
        scratch_shapes=[pltpu.VMEM((nb, CONV_PAD + ts, D_FF), F32)],
        compiler_params=pltpu.CompilerParams(
            dimension_semantics=("arbitrary",),
            vmem_limit_bytes=VMEM_LIMIT_BYTES),
        name="channel_sample",
    )(*args)


MIX_TILES = (256, 128)
FFN_TILES = (256, 128)


def _time_tile(seq, candidates):
    for tt in candidates:
        if seq % tt == 0:
            return tt
    raise ValueError(f"prompt length {seq} must be a multiple of {candidates[-1]}")


def kernel(x_prompt, x_sample, c_prompt, c_sample, cache_k_win, cache_v_win, state_pool, state_ffn_conv,
           norm1_g, norm2_g, w_ada, b_ada, w_in, attn_sink, w_o_attn, pool_w, pool_scale, w_o_pool,
           sgu_norm_g, sgu_w, sgu_b, w_o_sgu, w_out, ffn_w_up, ffn_conv_w, ffn_conv_b, ffn_w_down,
           final_norm_g):
    depth = w_in.shape[0]
    bp, seq, d = x_prompt.shape
    bs, ts, _ = x_sample.shape
    wc = cache_k_win.shape[2]
    assert d == D_MODEL and bp + bs <= MOD_ROWS
    assert seq >= WINDOW and ts % SUBLANES == 0 and ts >= POOL_STATE and wc + ts <= 2 * LANES
    tt_mix = _time_tile(seq, MIX_TILES)
    tt_ffn = _time_tile(seq, FFN_TILES)

    c_all = jnp.concatenate(
        [c_prompt, c_sample, jnp.zeros((MOD_ROWS - bp - bs, d), F32)], axis=0)
    mod = _modulation(c_all, w_ada, b_ada)
    mod_p = mod[:, 0:bp].reshape(depth, bp, 1, 6 * d)
    mod_s = mod[:, bp:bp + bs].reshape(depth, bs, 1, 6 * d)

    bf = lambda w: w.astype(BF16)
    params = dict(
        norm1_g=norm1_g.reshape(depth, 1, d), norm2_g=norm2_g.reshape(depth, 1, d),
        w_in=bf(w_in), attn_sink=attn_sink, w_o_attn=bf(w_o_attn),
        w_pool=_fold_pool_weights(pool_w, pool_scale, w_o_pool),
        sgu_norm_g=sgu_norm_g.reshape(depth, 1, SGU_WIDTH), sgu_w=bf(sgu_w),
        sgu_b_t=jnp.transpose(sgu_b, (0, 2, 1)), w_o_sgu=bf(w_o_sgu), w_out=bf(w_out),
        ffn_w_up=bf(ffn_w_up), ffn_conv_w=ffn_conv_w,
        ffn_conv_b=ffn_conv_b.reshape(depth, 1, D_FF), ffn_w_down=bf(ffn_w_down))
    final_g = final_norm_g.reshape(1, d)

    ck = cache_k_win.reshape(depth, bs, wc, KV_WIDTH)
    cv = cache_v_win.reshape(depth, bs, wc, KV_WIDTH)

    xp, xs = x_prompt, x_sample
    kp, vp, pp, cp = [], [], [], []
    ks_, vs_, ps_, cs_, ss_ = [], [], [], [], []
    for l in range(depth):
        final = l == depth - 1
        xp, nk, nv, npool = _tokmix_prompt(xp, mod_p[l], params, l, tt_mix)
        xp, nconv = _channel_prompt(xp, mod_p[l], params, l, final_g, tt_ffn, final)
        kp.append(nk); vp.append(nv); pp.append(npool); cp.append(nconv)
        xs, nk, nv, npool, nsv = _tokmix_sample(xs, mod_s[l], ck[l], cv[l], state_pool[l], params, l,
                                                PAST_LEN)
        xs, nconv = _channel_sample(xs, mod_s[l], state_ffn_conv[l], params, l, final_g, final)
        ks_.append(nk); vs_.append(nv); ps_.append(npool); cs_.append(nconv); ss_.append(nsv)

    kv_shape_p = (depth, bp, WINDOW, N_KV_HEADS, HEAD_DIM)
    kv_shape_s = (depth, bs, wc, N_KV_HEADS, HEAD_DIM)
    return (xp, xs,
            jnp.stack(kp).reshape(kv_shape_p), jnp.stack(vp).reshape(kv_shape_p),
            jnp.stack(pp), jnp.stack(cp),
            jnp.stack(ks_).reshape(kv_shape_s), jnp.stack(vs_).reshape(kv_shape_s),
            jnp.stack(ps_), jnp.stack(cs_), jnp.stack(ss_))
```

```python
import functools
import math

import jax
import jax.numpy as jnp
from jax import lax
from jax.experimental import pallas as pl
from jax.experimental.pallas import tpu as pltpu

F32 = jnp.float32
BF16 = jnp.bfloat16

D_MODEL = 1024
CHUNK = 64
EPS = 1e-6
N_HEADS = 16
N_KV_HEADS = 2
GROUP = N_HEADS // N_KV_HEADS
HEAD_DIM = 64
WINDOW = 128
WINDOW_CHUNKS = WINDOW // CHUNK
ATTN_WIDTH = N_HEADS * HEAD_DIM
KV_WIDTH = N_KV_HEADS * HEAD_DIM
POOL_WINDOWS = (2, 4, 8, 16)
POOL_GROUPS = 4
POOL_GROUP_WIDTH = D_MODEL // 8
POOL_WIDTH = POOL_GROUPS * POOL_GROUP_WIDTH
POOL_STATE = 16 - 1
SGU_LEN = 128
SGU_GROUPS = 4
SGU_GROUP_WIDTH = D_MODEL // 8
SGU_WIDTH = SGU_GROUPS * SGU_GROUP_WIDTH
N_BRANCH = 3
Q_END = ATTN_WIDTH
K_END = Q_END + KV_WIDTH
V_END = K_END + KV_WIDTH
POOL_END = V_END + POOL_WIDTH
SU_END = POOL_END + SGU_WIDTH
SV_END = SU_END + SGU_WIDTH
N_IN = SV_END + N_BRANCH * D_MODEL
D_FF = 2816
CONV_W = 3
PAST_LEN = 1024

LANES = 128
SUBLANES = 8
HALF_LANES = LANES // 2
VMEM_LIMIT_BYTES = 56 * 1024 * 1024

NEG_BIG = -1e30
GELU_C1 = -2.0 * math.sqrt(2.0 / math.pi) * math.log2(math.e)
GELU_C3 = GELU_C1 * 0.044715
PAIRS = GROUP // 2
KEY_WIN = 4 * CHUNK
PREV_ROWS = KEY_WIN - CHUNK
POOL_PAD = 16
CONV_PAD = SUBLANES
MOD_ROWS = 16
PROJ_SLAB = 2 * 256


def _alibi_slope(head):
    return 2.0 ** (-8.0 * (head + 1) / N_HEADS)


def _rmsnorm(x, g):
    return x * lax.rsqrt(jnp.mean(x * x, axis=-1, keepdims=True) + EPS) * g


def _gelu(x):
    return x / (1.0 + jnp.exp2(x * (GELU_C1 + GELU_C3 * (x * x))))


def _dot(a, b):
    return jnp.dot(a, b, preferred_element_type=F32)


def _dot_nt(a, b):
    return lax.dot_general(a, b, (((1,), (1,)), ((), ())), preferred_element_type=F32)


def _mod_kernel(c_ref, w_ref, b_ref, o_ref):
    c = c_ref[...]
    sc = c * jax.nn.sigmoid(c)
    o_ref[0] = jnp.dot(sc, w_ref[0], preferred_element_type=F32,
                       precision=lax.Precision.HIGHEST) + b_ref[0]


def _modulation(c_all, w_ada, b_ada):
    depth, d, n = w_ada.shape
    bn = n // 4
    return pl.pallas_call(
        _mod_kernel,
        grid=(depth, n // bn),
        in_specs=[
            pl.BlockSpec((MOD_ROWS, d), lambda l, j: (0, 0)),
            pl.BlockSpec((1, d, bn), lambda l, j: (l, 0, j)),
            pl.BlockSpec((1, 1, bn), lambda l, j: (l, 0, j)),
        ],
        out_specs=pl.BlockSpec((1, MOD_ROWS, bn), lambda l, j: (l, 0, j)),
        out_shape=jax.ShapeDtypeStruct((depth, MOD_ROWS, n), F32),
        compiler_params=pltpu.CompilerParams(
            dimension_semantics=("arbitrary", "arbitrary"),
            vmem_limit_bytes=VMEM_LIMIT_BYTES),
        name="adaln_modulation",
    )(c_all, w_ada, b_ada.reshape(depth, 1, n))


def _pool_fold_kernel(pw_ref, ps_ref, wo_ref, o_ref):
    for g in range(POOL_GROUPS):
        r0, r1 = g * POOL_GROUP_WIDTH, (g + 1) * POOL_GROUP_WIDTH
        scaled = pw_ref[g] * ps_ref[:, r0:r1]
        o_ref[r0:r1, :] = jnp.dot(scaled, wo_ref[r0:r1, :], preferred_element_type=F32,
                                  precision=lax.Precision.HIGHEST).astype(BF16)


def _fold_pool_weights(pool_w, pool_scale, w_o_pool):
    depth, _, d = w_o_pool.shape
    return pl.pallas_call(
        _pool_fold_kernel,
        grid=(depth,),
        in_specs=[
            pl.BlockSpec((None, POOL_GROUPS, POOL_GROUP_WIDTH, POOL_GROUP_WIDTH), lambda l: (l, 0, 0, 0)),
            pl.BlockSpec((None, 1, POOL_WIDTH), lambda l: (l, 0, 0)),
            pl.BlockSpec((None, POOL_WIDTH, d), lambda l: (l, 0, 0)),
        ],
        out_specs=pl.BlockSpec((None, POOL_WIDTH, d), lambda l: (l, 0, 0)),
        out_shape=jax.ShapeDtypeStruct((depth, POOL_WIDTH, d), BF16),
        compiler_params=pltpu.CompilerParams(
            dimension_semantics=("arbitrary",),
            vmem_limit_bytes=VMEM_LIMIT_BYTES),
        name="pool_weight_fold",
    )(pool_w, pool_scale.reshape(depth, 1, POOL_WIDTH), w_o_pool)


def _split_heads(z):
    lane = lax.broadcasted_iota(jnp.int32, z.shape, 1)
    low = lane < HALF_LANES
    zs = pltpu.roll(z, HALF_LANES, axis=1)
    zero = jnp.zeros_like(z)
    top0 = jnp.where(low, z, zero)
    bot0 = jnp.where(low, zero, zs)
    top1 = jnp.where(low, zs, zero)
    bot1 = jnp.where(low, zero, z)
    return top0, bot0, top1, bot1


def _zero_after(*results):
    bits = None
    for r in results:
        b = lax.bitcast_convert_type(r[0:SUBLANES, 0:LANES], jnp.uint32)
        bits = b if bits is None else bits | b
    bits = lax.shift_right_logical(lax.shift_right_logical(bits, jnp.uint32(16)), jnp.uint32(16))
    return lax.bitcast_convert_type(bits, F32)[0:1, 0:1]


def _attn_logits(kt, qs):
    return _dot_nt(kt, qs)


def _attn_softmax(logits_t, bias_t, sink_a, sink_b):
    w = logits_t.shape[0] // 2
    logits_t = logits_t + bias_t
    la = logits_t[:w, :]
    lb = logits_t[w:, :]
    ma = jnp.maximum(jnp.max(la, axis=0, keepdims=True), sink_a)
    mb = jnp.maximum(jnp.max(lb, axis=0, keepdims=True), sink_b)
    pa = jnp.exp(la - ma)
    pb = jnp.exp(lb - mb)
    da = jnp.sum(pa, axis=0, keepdims=True) + jnp.exp(sink_a - ma)
    db = jnp.sum(pb, axis=0, keepdims=True) + jnp.exp(sink_b - mb)
    return jnp.concatenate([pa, pb], axis=0).astype(BF16), da, db


def _attn_values(p_t, da, db, vb_t):
    o_t = _dot(vb_t, p_t)
    row = lax.broadcasted_iota(jnp.int32, o_t.shape, 0)
    o_t = o_t * jnp.where(row < HALF_LANES, 1.0 / da, 1.0 / db)
    return o_t.T


def _pool_window_sums(e):
    outs = []
    for gi, w in enumerate(POOL_WINDOWS):
        s = e[:, gi * POOL_GROUP_WIDTH:(gi + 1) * POOL_GROUP_WIDTH]
        k = 1
        while k < w:
            s = s + pltpu.roll(s, k, axis=0)
            k *= 2
        outs.append(s)
    return outs


def _pool_deltas(sums, pu, pos):
    outs = []
    for gi, w in enumerate(POOL_WINDOWS):
        c0, c1 = gi * POOL_GROUP_WIDTH, (gi + 1) * POOL_GROUP_WIDTH
        cnt = jnp.minimum(pos + 1, w).astype(F32)
        outs.append(sums[gi] / cnt - pu[:, c0:c1])
    return jnp.concatenate(outs, axis=1)


def _sgu_weight(sguw_ref, g, length):
    i = lax.broadcasted_iota(jnp.int32, (length, length), 0)
    j = lax.broadcasted_iota(jnp.int32, (length, length), 1)
    w = sguw_ref[g, :length, :length]
    return jnp.where((j // CHUNK) <= (i // CHUNK), w, jnp.zeros_like(w))


def _gated_sum(zg, branches):
    acc = None
    for i, v in enumerate(branches):
        term = v + jnp.tanh(0.5 * zg[:, i * D_MODEL:(i + 1) * D_MODEL]) * v
        acc = term if acc is None else acc + term
    return 0.5 * acc


def _merge_and_project(x, g1, zg, ya, pd, yc, woa_ref, wpool_ref, wos_ref, wout_ref):
    merged = _gated_sum(zg, [_dot(ya, woa_ref[...]),
                             _dot(pd.astype(BF16), wpool_ref[...]),
                             _dot(yc.astype(BF16), wos_ref[...])])
    return x + g1 * _dot(merged.astype(BF16), wout_ref[...])


def _build_prompt_bias(bias_tab):
    j = lax.broadcasted_iota(jnp.int32, (KEY_WIN, CHUNK), 0)
    q = lax.broadcasted_iota(jnp.int32, (KEY_WIN, CHUNK), 1)
    dist = jnp.abs(PREV_ROWS + q - j).astype(F32)
    for variant, first_valid in enumerate((3 * CHUNK, 2 * CHUNK, CHUNK)):
        valid = j >= first_valid
        for kv in range(N_KV_HEADS):
            for par in range(2):
                blk = jnp.concatenate(
                    [jnp.where(valid, -_alibi_slope(kv * GROUP + 2 * p + par) * dist, NEG_BIG)
                     for p in range(PAIRS)], axis=1)
                bias_tab[variant, kv, par * KEY_WIN:(par + 1) * KEY_WIN, :] = blk


def _tokmix_prompt_kernel(x_ref, mod_ref, n1g_ref, win_ref, woa_ref, wpool_ref, wos_ref, wout_ref,
                          sgug_ref, sguw_ref, sgub_ref, sink_ref,
                          xo_ref, nk_ref, nv_ref, npool_ref,
                          kext, vext, pext, ya_scr, bias_tab, *, tt):
    b = pl.program_id(0)
    t = pl.program_id(1)
    n_chunks = tt // CHUNK

    @pl.when(jnp.logical_and(b == 0, t == 0))
    def _():
        _build_prompt_bias(bias_tab)

    @pl.when(t == 0)
    def _():
        kext[:, 0:PREV_ROWS, :] = jnp.zeros((4, PREV_ROWS, LANES), BF16)
        vext[:, 0:PREV_ROWS, :] = jnp.zeros((4, PREV_ROWS, LANES), BF16)
        pext[0:POOL_PAD, :] = jnp.zeros((POOL_PAD, POOL_WIDTH), F32)

    x = x_ref[0]
    sh1 = mod_ref[0, :, 0:D_MODEL]
    sc1 = mod_ref[0, :, D_MODEL:2 * D_MODEL]
    g1 = mod_ref[0, :, 2 * D_MODEL:3 * D_MODEL]
    h = _rmsnorm(x, n1g_ref[...]) * (1.0 + sc1) + sh1
    hb = h.astype(BF16)

    zqkv = _dot(hb, win_ref[:, 0:V_END])
    zk = zqkv[:, Q_END:K_END]
    zv = zqkv[:, K_END:V_END]
    nk_ref[0] = zk[tt - WINDOW:, :]
    nv_ref[0] = zv[tt - WINDOW:, :]
    qb = (zqkv[:, 0:Q_END] * (HEAD_DIM ** -0.5)).astype(BF16)
    for idx, part in enumerate(_split_heads(zk)):
        kext[idx, PREV_ROWS:PREV_ROWS + tt, :] = part.astype(BF16)
    for idx, part in enumerate(_split_heads(zv)):
        vext[idx, PREV_ROWS:PREV_ROWS + tt, :] = part.astype(BF16)

    slab_starts = list(range(V_END, N_IN, PROJ_SLAB))
    slabs = []

    def project_slab():
        c0 = slab_starts[len(slabs)]
        slabs.append(_dot(hb, win_ref[:, c0:c0 + PROJ_SLAB]))
        return slabs[-1]

    def pool_projection():
        pu = slabs[0]
        npool_ref[0] = pu[tt - POOL_STATE:, :]
        pext[POOL_PAD:POOL_PAD + tt, :] = pu
        sums = [s[POOL_PAD:, :] for s in _pool_window_sums(pext[...])]
        pos = t * tt + lax.broadcasted_iota(jnp.int32, (tt, POOL_GROUP_WIDTH), 0)
        pd = _pool_deltas(sums, pu, pos)
        pext[0:POOL_PAD, :] = pext[tt:tt + POOL_PAD, :]
        return _dot(pd.astype(BF16), wpool_ref[...])

    def block_logits(i, kv, zero=None):
        r0 = i * CHUNK
        qs = jnp.concatenate(
            [qb[r0:r0 + CHUNK, (kv * PAIRS + p) * LANES:(kv * PAIRS + p + 1) * LANES]
             for p in range(PAIRS)], axis=0)
        kt = jnp.concatenate([kext[2 * kv, r0:r0 + KEY_WIN, :],
                              kext[2 * kv + 1, r0:r0 + KEY_WIN, :]], axis=0)
        if zero is not None:
            kt = kt + zero.astype(BF16)
        return _attn_logits(kt, qs)

    def block_softmax(i, kv, logits):
        variant = jnp.where(t == 0, min(i, 2), 2) if i < 2 else 2
        return _attn_softmax(logits, bias_tab[variant, kv], sink_ref[kv, 0], sink_ref[kv, 1])

    def block_values(i, kv, p_t, da, db, zero):
        r0 = i * CHUNK
        vb = jnp.concatenate([vext[2 * kv, r0:r0 + KEY_WIN, :],
                              vext[2 * kv + 1, r0:r0 + KEY_WIN, :]], axis=0)
        o = _attn_values(p_t, da, db, vb.T + zero.astype(BF16))
        for p in range(PAIRS):
            ya_scr[r0:r0 + CHUNK, (kv * PAIRS + p) * LANES:(kv * PAIRS + p + 1) * LANES] = (
                o[p * CHUNK:(p + 1) * CHUNK, :].astype(BF16))

    blocks = [(i, kv) for i in range(n_chunks) for kv in range(N_KV_HEADS)]
    n_lead = len(slab_starts) + 1 - len(blocks)
    assert n_lead >= 1
    for _ in range(n_lead):
        lead = project_slab()
    logits = block_logits(*blocks[0], _zero_after(lead))
    merged_b = None
    for n, blk in enumerate(blocks):
        p_t, da, db = block_softmax(*blk, logits)
        ahead = []
        logits = None
        if n + 1 < len(blocks):
            logits = block_logits(*blocks[n + 1])
            ahead.append(logits)
        if len(slabs) < len(slab_starts):
            ahead.append(project_slab())
        else:
            merged_b = pool_projection()
            ahead.append(merged_b)
        block_values(*blk, p_t, da, db, _zero_after(*ahead))

    kext[:, 0:PREV_ROWS, :] = kext[:, tt:tt + PREV_ROWS, :]
    vext[:, 0:PREV_ROWS, :] = vext[:, tt:tt + PREV_ROWS, :]

    zsgu = jnp.concatenate(slabs[1:3], axis=1)
    uu = _gelu(zsgu[:, 0:SGU_WIDTH])
    vn = _rmsnorm(_gelu(zsgu[:, SGU_WIDTH:]), sgug_ref[...])
    vnb = vn.astype(BF16)
    n_sgu = tt // SGU_LEN
    yc_cols = []
    for g in range(SGU_GROUPS):
        c0, c1 = g * SGU_GROUP_WIDTH, (g + 1) * SGU_GROUP_WIDTH
        rhs = jnp.concatenate([vnb[n * SGU_LEN:(n + 1) * SGU_LEN, c0:c1] for n in range(n_sgu)], axis=1)
        s = _dot(_sgu_weight(sguw_ref, g, SGU_LEN), rhs) + sgub_ref[:, g:g + 1]
        s_rows = jnp.concatenate([s[:, n * SGU_GROUP_WIDTH:(n + 1) * SGU_GROUP_WIDTH]
                                  for n in range(n_sgu)], axis=0)
        yc_cols.append(uu[:, c0:c1] * s_rows)
    yc = jnp.concatenate(yc_cols, axis=1)

    zg = jnp.concatenate(slabs[3:], axis=1)
    merged = _gated_sum(zg, [_dot(ya_scr[...], woa_ref[...]), merged_b,
                             _dot(yc.astype(BF16), wos_ref[...])])
    xo_ref[0] = x + g1 * _dot(merged.astype(BF16), wout_ref[...])


def _resident(shape, layer=None):
    if layer is None:
        return pl.BlockSpec(shape, lambda *_: (0,) * len(shape), pipeline_mode=pl.Buffered(1))
    return pl.BlockSpec((None,) + tuple(shape), lambda *_: (layer,) + (0,) * len(shape),
                        pipeline_mode=pl.Buffered(1))


def _tokmix_prompt(x, mod, lw, layer, tt):
    bsz, seq, d = x.shape
    nt = seq // tt
    rows = PAIRS * CHUNK
    kernel = functools.partial(_tokmix_prompt_kernel, tt=tt)
    out_shape = (
        jax.ShapeDtypeStruct((bsz, seq, d), F32),
        jax.ShapeDtypeStruct((bsz, WINDOW, KV_WIDTH), F32),
        jax.ShapeDtypeStruct((bsz, WINDOW, KV_WIDTH), F32),
        jax.ShapeDtypeStruct((bsz, POOL_STATE, POOL_WIDTH), F32),
    )
    return pl.pallas_call(
        kernel,
        grid=(bsz, nt),
        in_specs=[
            pl.BlockSpec((1, tt, d), lambda b, t: (b, t, 0)),
            pl.BlockSpec((1, 1, 6 * d), lambda b, t: (b, 0, 0)),
            _resident((1, d), layer),
            _resident((d, N_IN), layer),
            _resident((ATTN_WIDTH, d), layer),
            _resident((POOL_WIDTH, d), layer),
            _resident((SGU_WIDTH, d), layer),
            _resident((d, d), layer),
            _resident((1, SGU_WIDTH), layer),
            _resident((SGU_GROUPS, SGU_LEN, SGU_LEN), layer),
            _resident((SGU_LEN, SGU_GROUPS), layer),
            _resident((N_KV_HEADS, 2, 1, rows)),
        ],
        out_specs=(
            pl.BlockSpec((1, tt, d), lambda b, t: (b, t, 0)),
            pl.BlockSpec((1, WINDOW, KV_WIDTH), lambda b, t: (b, 0, 0)),
            pl.BlockSpec((1, WINDOW, KV_WIDTH), lambda b, t: (b, 0, 0)),
            pl.BlockSpec((1, POOL_STATE, POOL_WIDTH), lambda b, t: (b, 0, 0)),
        ),
        out_shape=out_shape,
        scratch_shapes=[
            pltpu.VMEM((4, PREV_ROWS + tt, LANES), BF16),
            pltpu.VMEM((4, PREV_ROWS + tt, LANES), BF16),
            pltpu.VMEM((POOL_PAD + tt, POOL_WIDTH), F32),
            pltpu.VMEM((tt, ATTN_WIDTH), BF16),
            pltpu.VMEM((3, N_KV_HEADS, 2 * KEY_WIN, rows), F32),
        ],
        compiler_params=pltpu.CompilerParams(
            dimension_semantics=("arbitrary", "arbitrary"),
            vmem_limit_bytes=VMEM_LIMIT_BYTES),
        name="tokmix_prompt",
    )(x, mod, lw["norm1_g"], lw["w_in"], lw["w_o_attn"], lw["w_pool"], lw["w_o_sgu"], lw["w_out"],
      lw["sgu_norm_g"], lw["sgu_w"], lw["sgu_b_t"],
      _sink_table(lw["attn_sink"][layer], CHUNK))


def _sink_table(sink, rows_per_pair):
    s = sink.astype(F32).reshape(N_KV_HEADS, PAIRS, 2)
    s = jnp.transpose(s, (0, 2, 1))
    s = jnp.repeat(s, rows_per_pair, axis=2)
    return s[:, :, None, :]


def _conv_ffn_tail(x, g2, a, a1, a2, bgate, cw_ref, cb, wdown_ref):
    conv = a2 * cw_ref[0:1, :] + a1 * cw_ref[1:2, :] + a * cw_ref[2:3, :] + cb
    f = _dot((_gelu(conv) * bgate).astype(BF16), wdown_ref[...])
    return x + g2 * f


def _channel_prompt_kernel(x_ref, mod_ref, n2g_ref, wup_ref, cw_ref, cb_ref, wdown_ref, fng_ref,
                           xo_ref, nconv_ref, ashift, *, tt, final):
    t = pl.program_id(1)

    @pl.when(t == 0)
    def _():
        for k in range(1, CONV_W):
            ashift[k - 1, 0:k, :] = jnp.zeros((k, D_FF), F32)

    x = x_ref[0]
    sh2 = mod_ref[0, :, 3 * D_MODEL:4 * D_MODEL]
    sc2 = mod_ref[0, :, 4 * D_MODEL:5 * D_MODEL]
    g2 = mod_ref[0, :, 5 * D_MODEL:6 * D_MODEL]
    h = _rmsnorm(x, n2g_ref[...]) * (1.0 + sc2) + sh2
    up = _dot(h.astype(BF16), wup_ref[...])
    a = up[:, 0:D_FF]
    bgate = up[:, D_FF:]
    nconv_ref[0] = a[tt - (CONV_W - 1):, :]
    delayed = []
    for k in range(1, CONV_W):
        ashift[k - 1, k:k + tt, :] = a
        delayed.append(ashift[k - 1, 0:tt, :])
    a1, a2 = delayed
    y = _conv_ffn_tail(x, g2, a, a1, a2, bgate, cw_ref, cb_ref[...], wdown_ref)
    for k in range(1, CONV_W):
        ashift[k - 1, 0:k, :] = ashift[k - 1, tt:tt + k, :]
    if final:
        y = _rmsnorm(y, fng_ref[...])
    xo_ref[0] = y


def _channel_prompt(x, mod, lw, layer, final_g, tt, final):
    bsz, seq, d = x.shape
    nt = seq // tt
    kernel = functools.partial(_channel_prompt_kernel, tt=tt, final=final)
    return pl.pallas_call(
        kernel,
        grid=(bsz, nt),
        in_specs=[
            pl.BlockSpec((1, tt, d), lambda b, t: (b, t, 0)),
            pl.BlockSpec((1, 1, 6 * d), lambda b, t: (b, 0, 0)),
            _resident((1, d), layer),
            _resident((d, 2 * D_FF), layer),
            _resident((CONV_W, D_FF), layer),
            _resident((1, D_FF), layer),
            _resident((D_FF, d), layer),
            _resident((1, d)),
        ],
        out_specs=(
            pl.BlockSpec((1, tt, d), lambda b, t: (b, t, 0)),
            pl.BlockSpec((1, CONV_W - 1, D_FF), lambda b, t: (b, 0, 0)),
        ),
        out_shape=(
            jax.ShapeDtypeStruct((bsz, seq, d), F32),
            jax.ShapeDtypeStruct((bsz, CONV_W - 1, D_FF), F32),
        ),
        scratch_shapes=[pltpu.VMEM((CONV_W - 1, tt + CONV_PAD, D_FF), F32)],
        compiler_params=pltpu.CompilerParams(
            dimension_semantics=("arbitrary", "arbitrary"),
            vmem_limit_bytes=VMEM_LIMIT_BYTES),
        name="channel_prompt",
    )(x, mod, lw["norm2_g"], lw["ffn_w_up"], lw["ffn_conv_w"], lw["ffn_conv_b"], lw["ffn_w_down"],
      final_g)


def _tokmix_sample_kernel(x_ref, mod_ref, ck_ref, cv_ref, sp_ref, n1g_ref, win_ref, woa_ref, wpool_ref,
                          wos_ref, wout_ref, sgug_ref, sguw_ref, sgub_ref,
                          sink_ref,
                          xo_ref, nk_ref, nv_ref, npool_ref, vn_ref,
                          ya_scr, *, pos0):
    nb, ts, d = x_ref.shape
    wc = ck_ref.shape[1]
    n_keys = wc + ts
    key_win = 2 * LANES

    x3 = x_ref[...]
    sh1 = mod_ref[:, :, 0:D_MODEL]
    sc1 = mod_ref[:, :, D_MODEL:2 * D_MODEL]
    g1 = mod_ref[:, :, 2 * D_MODEL:3 * D_MODEL]
    h3 = _rmsnorm(x3, n1g_ref[...]) * (1.0 + sc1) + sh1
    hb = h3.reshape(nb * ts, d).astype(BF16)

    zqkv = _dot(hb, win_ref[:, 0:V_END])
    qb = (zqkv[:, 0:Q_END] * (HEAD_DIM ** -0.5)).astype(BF16)

    j = lax.broadcasted_iota(jnp.int32, (key_win, ts), 0)
    q = lax.broadcasted_iota(jnp.int32, (key_win, ts), 1)
    dist = jnp.abs(wc + q - j).astype(F32)
    valid = jnp.logical_and(j < n_keys, (pos0 - wc + j) >= 0)
    bias = []
    for kv in range(N_KV_HEADS):
        blocks = []
        for par in range(2):
            row = [jnp.where(valid, -_alibi_slope(kv * GROUP + 2 * p + par) * dist, NEG_BIG)
                   for p in range(PAIRS)]
            blocks.append(jnp.concatenate(row, axis=1))
        bias.append(jnp.concatenate(blocks, axis=0))

    pad = jnp.zeros((key_win - n_keys, KV_WIDTH), F32)
    for b in range(nb):
        r0 = b * ts
        zk = zqkv[r0:r0 + ts, Q_END:K_END]
        zv = zqkv[r0:r0 + ts, K_END:V_END]
        ck = ck_ref[b]
        cv = cv_ref[b]
        nk_ref[b] = jnp.concatenate([ck[ts:, :], zk], axis=0)
        nv_ref[b] = jnp.concatenate([cv[ts:, :], zv], axis=0)
        ksplit = [part.astype(BF16) for part in _split_heads(jnp.concatenate([ck, zk, pad], axis=0))]
        vsplit = [part.astype(BF16) for part in _split_heads(jnp.concatenate([cv, zv, pad], axis=0))]
        for kv in range(N_KV_HEADS):
            qs = jnp.concatenate(
                [qb[r0:r0 + ts, (kv * PAIRS + p) * LANES:(kv * PAIRS + p + 1) * LANES]
                 for p in range(PAIRS)], axis=0)
            kt = jnp.concatenate([ksplit[2 * kv], ksplit[2 * kv + 1]], axis=0)
            vb = jnp.concatenate([vsplit[2 * kv], vsplit[2 * kv + 1]], axis=0)
            p_t, da, db = _attn_softmax(_attn_logits(kt, qs), bias[kv], sink_ref[kv, 0], sink_ref[kv, 1])
            o = _attn_values(p_t, da, db, vb.T)
            for p in range(PAIRS):
                ya_scr[r0:r0 + ts, (kv * PAIRS + p) * LANES:(kv * PAIRS + p + 1) * LANES] = (
                    o[p * ts:(p + 1) * ts, :].astype(BF16))

    zmix = _dot(hb, win_ref[:, V_END:SV_END])
    pu = zmix[:, 0:POOL_WIDTH]
    head = jnp.zeros((POOL_PAD - POOL_STATE, POOL_WIDTH), F32)
    sums = [[] for _ in POOL_WINDOWS]
    for b in range(nb):
        pub = pu[b * ts:(b + 1) * ts, :]
        npool_ref[b] = pub[ts - POOL_STATE:, :]
        e = jnp.concatenate([head, sp_ref[b], pub], axis=0)
        for gi, s in enumerate(_pool_window_sums(e)):
            sums[gi].append(s[POOL_PAD:, :])
    sums = [jnp.concatenate(s, axis=0) for s in sums]
    pos = pos0 + lax.broadcasted_iota(jnp.int32, (nb, ts, POOL_GROUP_WIDTH), 1).reshape(
        nb * ts, POOL_GROUP_WIDTH)
    pd = _pool_deltas(sums, pu, pos)

    uu = _gelu(zmix[:, POOL_WIDTH:POOL_WIDTH + SGU_WIDTH])
    vn = _rmsnorm(_gelu(zmix[:, POOL_WIDTH + SGU_WIDTH:]), sgug_ref[...])
    vn_ref[...] = vn.reshape(nb, ts, SGU_WIDTH)
    vnb = vn.astype(BF16)
    yc_cols = []
    for g in range(SGU_GROUPS):
        c0, c1 = g * SGU_GROUP_WIDTH, (g + 1) * SGU_GROUP_WIDTH
        rhs = jnp.concatenate([vnb[b * ts:(b + 1) * ts, c0:c1] for b in range(nb)], axis=1)
        s = _dot(_sgu_weight(sguw_ref, g, ts), rhs) + sgub_ref[0:ts, g:g + 1]
        s_rows = jnp.concatenate([s[:, b * SGU_GROUP_WIDTH:(b + 1) * SGU_GROUP_WIDTH]
                                  for b in range(nb)], axis=0)
        yc_cols.append(uu[:, c0:c1] * s_rows)
    yc = jnp.concatenate(yc_cols, axis=1)

    x2 = x3.reshape(nb * ts, d)
    g1r = jnp.broadcast_to(g1, (nb, ts, d)).reshape(nb * ts, d)
    zg = _dot(hb, win_ref[:, SV_END:N_IN])
    y = _merge_and_project(x2, g1r, zg, ya_scr[...], pd, yc, woa_ref, wpool_ref, wos_ref, wout_ref)
    xo_ref[...] = y.reshape(nb, ts, d)


def _full(shape, layer=None):
    if layer is None:
        return pl.BlockSpec(shape, lambda *_: (0,) * len(shape))
    return pl.BlockSpec((None,) + tuple(shape[1:]), lambda *_: (layer,) + (0,) * (len(shape) - 1))


def _tokmix_sample(x, mod, cache_k, cache_v, state_pool, lw, layer, pos0):
    nb, ts, d = x.shape
    wc = cache_k.shape[1]
    kernel = functools.partial(_tokmix_sample_kernel, pos0=pos0)
    per_layer = (lw["norm1_g"], lw["w_in"], lw["w_o_attn"], lw["w_pool"], lw["w_o_sgu"], lw["w_out"],
                 lw["sgu_norm_g"], lw["sgu_w"], lw["sgu_b_t"])
    shared = (x, mod, cache_k, cache_v, state_pool)
    sink = _sink_table(lw["attn_sink"][layer], ts)
    args = shared + per_layer + (sink,)
    in_specs = ([_full(a.shape) for a in shared] + [_full(a.shape, layer) for a in per_layer]
                + [_full(sink.shape)])
    out_shape = (
        jax.ShapeDtypeStruct((nb, ts, d), F32),
        jax.ShapeDtypeStruct((nb, wc, KV_WIDTH), F32),
        jax.ShapeDtypeStruct((nb, wc, KV_WIDTH), F32),
        jax.ShapeDtypeStruct((nb, POOL_STATE, POOL_WIDTH), F32),
        jax.ShapeDtypeStruct((nb, ts, SGU_WIDTH), F32),
    )
    return pl.pallas_call(
        kernel,
        grid=(1,),
        in_specs=in_specs,
        out_specs=tuple(_full(s.shape) for s in out_shape),
        out_shape=out_shape,
        scratch_shapes=[pltpu.VMEM((nb * ts, ATTN_WIDTH), BF16)],
        compiler_params=pltpu.CompilerParams(
            dimension_semantics=("arbitrary",),
            vmem_limit_bytes=VMEM_LIMIT_BYTES),
        name="tokmix_sample",
    )(*args)


def _channel_sample_kernel(x_ref, mod_ref, sc_ref, n2g_ref, wup_ref, cw_ref, cb_ref, wdown_ref, fng_ref,
                           xo_ref, nconv_ref, aext, *, final):
    nb, ts, d = x_ref.shape
    x3 = x_ref[...]
    sh2 = mod_ref[:, :, 3 * D_MODEL:4 * D_MODEL]
    sc2 = mod_ref[:, :, 4 * D_MODEL:5 * D_MODEL]
    g2 = mod_ref[:, :, 5 * D_MODEL:6 * D_MODEL]
    h3 = _rmsnorm(x3, n2g_ref[...]) * (1.0 + sc2) + sh2
    up = _dot(h3.reshape(nb * ts, d).astype(BF16), wup_ref[...])
    a = up[:, 0:D_FF]
    bgate = up[:, D_FF:]
    a1s, a2s = [], []
    for b in range(nb):
        ab = a[b * ts:(b + 1) * ts, :]
        nconv_ref[b] = ab[ts - (CONV_W - 1):, :]
        aext[b, CONV_PAD - (CONV_W - 1):CONV_PAD, :] = sc_ref[b]
        aext[b, CONV_PAD:CONV_PAD + ts, :] = ab
        a1s.append(aext[b, CONV_PAD - 1:CONV_PAD - 1 + ts, :])
        a2s.append(aext[b, CONV_PAD - 2:CONV_PAD - 2 + ts, :])
    a1 = jnp.concatenate(a1s, axis=0)
    a2 = jnp.concatenate(a2s, axis=0)
    x2 = x3.reshape(nb * ts, d)
    g2r = jnp.broadcast_to(g2, (nb, ts, d)).reshape(nb * ts, d)
    y = _conv_ffn_tail(x2, g2r, a, a1, a2, bgate, cw_ref, cb_ref[...], wdown_ref)
    if final:
        y = _rmsnorm(y, fng_ref[...])
    xo_ref[...] = y.reshape(nb, ts, d)


def _channel_sample(x, mod, state_conv, lw, layer, final_g, final):
    nb, ts, d = x.shape
    kernel = functools.partial(_channel_sample_kernel, final=final)
    per_layer = (lw["norm2_g"], lw["ffn_w_up"], lw["ffn_conv_w"], lw["ffn_conv_b"], lw["ffn_w_down"])
    shared = (x, mod, state_conv)
    args = shared + per_layer + (final_g,)
    in_specs = ([_full(a.shape) for a in shared] + [_full(a.shape, layer) for a in per_layer]
                + [_full(final_g.shape)])
    out_shape = (
        jax.ShapeDtypeStruct((nb, ts, d), F32),
        jax.ShapeDtypeStruct((nb, CONV_W - 1, D_FF), F32),
    )
    return pl.pallas_call(
        kernel,
        grid=(1,),
        in_specs=in_specs,
        out_specs=tuple(_full(s.shape) for s in out_shape),
        out_shape=out_shape,
        scratch_shapes=[pltpu.VMEM((nb, CONV_PAD + ts, D_FF), F32)],
        compiler_params=pltpu.CompilerParams(
            dimension_semantics=("arbitrary",),
            vmem_limit_bytes=VMEM_LIMIT_BYTES),
        name="channel_sample",
    )(*args)


MIX_TILES = (256, 128)
FFN_TILES = (256, 128)


def _time_tile(seq, candidates):
    for tt in candidates:
        if seq % tt == 0:
            return tt
    raise ValueError(f"prompt length {seq} must be a multiple of {candidates[-1]}")


def kernel(x_prompt, x_sample, c_prompt, c_sample, cache_k_win, cache_v_win, state_pool, state_ffn_conv,
           norm1_g, norm2_g, w_ada, b_ada, w_in, attn_sink, w_o_attn, pool_w, pool_scale, w_o_pool,
           sgu_norm_g, sgu_w, sgu_b, w_o_sgu, w_out, ffn_w_up, ffn_conv_w, ffn_conv_b, ffn_w_down,
           final_norm_g):
    depth = w_in.shape[0]
    bp, seq, d = x_prompt.shape
    bs, ts, _ = x_sample.shape
    wc = cache_k_win.shape[2]
    assert d == D_MODEL and bp + bs <= MOD_ROWS
    assert seq >= WINDOW and ts % SUBLANES == 0 and ts >= POOL_STATE and wc + ts <= 2 * LANES
    tt_mix = _time_tile(seq, MIX_TILES)
    tt_ffn = _time_tile(seq, FFN_TILES)

    c_all = jnp.concatenate(
        [c_prompt, c_sample, jnp.zeros((MOD_ROWS - bp - bs, d), F32)], axis=0)
    mod = _modulation(c_all, w_ada, b_ada)
    mod_p = mod[:, 0:bp].reshape(depth, bp, 1, 6 * d)
    mod_s = mod[:, bp:bp + bs].reshape(depth, bs, 1, 6 * d)

    bf = lambda w: w.astype(BF16)
    params = dict(
        norm1_g=norm1_g.reshape(depth, 1, d), norm2_g=norm2_g.reshape(depth, 1, d),
        w_in=bf(w_in), attn_sink=attn_sink, w_o_attn=bf(w_o_attn),
        w_pool=_fold_pool_weights(pool_w, pool_scale, w_o_pool),
        sgu_norm_g=sgu_norm_g.reshape(depth, 1, SGU_WIDTH), sgu_w=bf(sgu_w),
        sgu_b_t=jnp.transpose(sgu_b, (0, 2, 1)), w_o_sgu=bf(w_o_sgu), w_out=bf(w_out),
        ffn_w_up=bf(ffn_w_up), ffn_conv_w=ffn_conv_w,
        ffn_conv_b=ffn_conv_b.reshape(depth, 1, D_FF), ffn_w_down=bf(ffn_w_down))
    final_g = final_norm_g.reshape(1, d)

    ck = cache_k_win.reshape(depth, bs, wc, KV_WIDTH)
    cv = cache_v_win.reshape(depth, bs, wc, KV_WIDTH)

    xp, xs = x_prompt, x_sample
    kp, vp, pp, cp = [], [], [], []
    ks_, vs_, ps_, cs_, ss_ = [], [], [], [], []
    for l in range(depth):
        final = l == depth - 1
        xp, nk, nv, npool = _tokmix_prompt(xp, mod_p[l], params, l, tt_mix)
        xp, nconv = _channel_prompt(xp, mod_p[l], params, l, final_g, tt_ffn, final)
        kp.append(nk); vp.append(nv); pp.append(npool); cp.append(nconv)
        xs, nk, nv, npool, nsv = _tokmix_sample(xs, mod_s[l], ck[l], cv[l], state_pool[l], params, l,
                                                PAST_LEN)
        xs, nconv = _channel_sample(xs, mod_s[l], state_ffn_conv[l], params, l, final_g, final)
        ks_.append(nk); vs_.append(nv); ps_.append(npool); cs_.append(nconv); ss_.append(nsv)

    kv_shape_p = (depth, bp, WINDOW, N_KV_HEADS, HEAD_DIM)
    kv_shape_s = (depth, bs, wc, N_KV_HEADS, HEAD_DIM)
    return (xp, xs,
            jnp.stack(kp).reshape(kv_shape_p), jnp.stack(vp).reshape(kv_shape_p),
            jnp.stack(pp), jnp.stack(cp),
            jnp.stack(ks_).reshape(kv_shape_s), jnp.stack(vs_).reshape(kv_shape_s),
            jnp.stack(ps_), jnp.stack(cs_), jnp.stack(ss_))
```
